```python
import jax
import jax.numpy as jnp
from jax import lax
import numpy as np


D_MODEL = 4096
BATCH = 2
SEQ = 8192
DEPTH = 2

N_MIXERS = 2
N_ATTN_LAYERS = (DEPTH + 1) // 2
N_HGRN_LAYERS = DEPTH // 2
ATTN_HEADS = 32
ATTN_KV_HEADS = 8
ATTN_HEAD_DIM = D_MODEL // ATTN_HEADS
ATTN_GROUP = ATTN_HEADS // ATTN_KV_HEADS
WINDOW = 128
BLOCK = 128
HGRN_HEADS = 32
HGRN_EXPAND = 128
HGRN_FORGET_DIM = HGRN_HEADS * HGRN_EXPAND
HGRN_HEAD_V = D_MODEL // HGRN_HEADS
HGRN_CHUNK = 64
D_FF = 2 * D_MODEL
N_MOD = 9
EPS = 1e-6

kernel_name = 'hybrid_swa_sink_hgrn2_macaron_adaln'


def rmsnorm(x, gain):
    xf = x.astype(jnp.float32)
    y = xf * lax.rsqrt(jnp.mean(xf * xf, axis=-1, keepdims=True) + EPS)
    return (y * gain.astype(jnp.float32)).astype(x.dtype)


def modulate(h, shift, scale):
    return h * (1 + scale[:, None, :]) + shift[:, None, :]


def swiglu(h, w_gu, w_down):
    a, b = jnp.split(h @ w_gu, 2, axis=-1)
    return (jax.nn.silu(a) * b) @ w_down


def alibi_slopes(n_heads):
    return jnp.exp2(-8.0 * jnp.arange(1, n_heads + 1, dtype=jnp.float32) / n_heads)


def sliding_window_attention(h, w_qkv, w_o, q_gain, k_gain, sinks):
    B, T, _ = h.shape
    nb = T // BLOCK
    HD, KV, G = ATTN_HEAD_DIM, ATTN_KV_HEADS, ATTN_GROUP
    q, k, v = jnp.split(h @ w_qkv, [ATTN_HEADS * HD, (ATTN_HEADS + KV) * HD], axis=-1)
    q = rmsnorm(q.reshape(B, T, ATTN_HEADS, HD), q_gain)
    k = rmsnorm(k.reshape(B, T, KV, HD), k_gain)
    v = v.reshape(B, T, KV, HD)
    q = q.reshape(B, nb, BLOCK, KV, G, HD)

    def with_prev(a):
        a = a.reshape(B, nb, BLOCK, KV, HD)
        prev = jnp.pad(a, ((0, 0), (1, 0), (0, 0), (0, 0), (0, 0)))[:, :-1]
        return jnp.concatenate([prev, a], axis=2)

    kb, vb = with_prev(k), with_prev(v)
    scores = jnp.einsum('bnqhgd,bnkhd->bnhgqk', q, kb).astype(jnp.float32) * (HD ** -0.5)
    qi = jnp.arange(BLOCK)[:, None]
    kj = jnp.arange(2 * BLOCK)[None, :]
    dist = qi + BLOCK - kj
    blk = jnp.arange(nb)[:, None, None]
    valid = (dist >= 0) & (dist < WINDOW) & ((blk > 0) | (kj >= BLOCK))
    slopes = alibi_slopes(ATTN_HEADS).reshape(KV, G)
    bias = -slopes[:, :, None, None] * dist.astype(jnp.float32)
    logits = jnp.where(valid[None, :, None, None], scores + bias, -jnp.inf)
    sink = sinks.astype(jnp.float32).reshape(KV, G)[:, :, None, None]
    m = jnp.maximum(jnp.max(logits, axis=-1, keepdims=True), sink)
    p = jnp.exp(logits - m)
    probs = p / (jnp.sum(p, axis=-1, keepdims=True) + jnp.exp(sink - m))
    out = jnp.einsum('bnhgqk,bnkhd->bnqhgd', probs.astype(vb.dtype), vb)
    return out.reshape(B, T, ATTN_HEADS * HD) @ w_o


def chunk_gated_recurrence(q, k, v, log_f):
    B, T, H, K = q.shape
    V = v.shape[-1]
    C = HGRN_CHUNK
    n = T // C

    def to_chunks(a):
        return a.reshape(B, n, C, H, a.shape[-1]).transpose(1, 0, 3, 2, 4)

    causal = jnp.tril(jnp.ones((C, C), dtype=bool))

    def step(S, inp):
        qc, kc, vc, gc = inp
        b = jnp.cumsum(gc, axis=2)
        inter = jnp.einsum('bhtk,bhkv->bhtv', qc * jnp.exp(b), S)
        rel = jnp.where(causal[:, :, None], b[:, :, :, None, :] - b[:, :, None, :, :], -jnp.inf)
        attn = jnp.einsum('bhtk,bhtsk,bhsk->bhts', qc, jnp.exp(rel), kc)
        intra = jnp.einsum('bhts,bhsv->bhtv', attn, vc)
        b_last = b[:, :, -1:, :]
        S = jnp.exp(b_last[:, :, 0, :])[..., None] * S + jnp.einsum('bhsk,bhsv->bhkv', kc * jnp.exp(b_last - b), vc)
        return S, inter + intra

    S0 = jnp.zeros((B, H, K, V), dtype=jnp.float32)
    _, o = lax.scan(step, S0, (to_chunks(q), to_chunks(k), to_chunks(v), to_chunks(log_f)))
    return o.transpose(1, 0, 3, 2, 4).reshape(B, T, H, V)


def hgrn2(h, w_qfig, w_o, g_gain, lb):
    B, T, _ = h.shape
    FD = HGRN_FORGET_DIM
    q, f, i, g = jnp.split(h @ w_qfig, [FD, 2 * FD, 2 * FD + D_MODEL], axis=-1)
    f = f.astype(jnp.float32)
    lb = lb.astype(jnp.float32)
    log_f = jnp.logaddexp(jnp.log(lb), jnp.log1p(-lb) + jax.nn.log_sigmoid(f))
    k = (1 - lb) * jax.nn.sigmoid(-f)
    q = jax.nn.silu(q.astype(jnp.float32))
    shp_k = (B, T, HGRN_HEADS, HGRN_EXPAND)
    shp_v = (B, T, HGRN_HEADS, HGRN_HEAD_V)
    o = chunk_gated_recurrence(q.reshape(shp_k), k.reshape(shp_k),
                               i.astype(jnp.float32).reshape(shp_v), log_f.reshape(shp_k))
    o = rmsnorm(o, g_gain) * jax.nn.silu(g.astype(jnp.float32).reshape(shp_v))
    return o.reshape(B, T, D_MODEL).astype(h.dtype) @ w_o


def setup_inputs(seed: int = 0) -> dict:
    key = jax.random.key(seed)
    ks = jax.random.split(key, 20)
    D, F = D_MODEL, D_FF
    nrm = jax.random.normal
    f32 = jnp.float32
    return {
        'x': nrm(ks[0], (BATCH, SEQ, D), f32),
        'c': nrm(ks[1], (BATCH, D), f32),
        'ada_w': nrm(ks[2], (DEPTH, D, N_MOD * D), f32) * (0.5 * D ** -0.5),
        'ada_b': nrm(ks[3], (DEPTH, N_MOD * D), f32) * 0.02,
        'norm_gains': 1.0 + 0.02 * nrm(ks[4], (DEPTH, 3, D), f32),
        'ffn1_w_gu': nrm(ks[5], (DEPTH, D, 2 * F), f32) * D ** -0.5,
        'ffn1_w_down': nrm(ks[6], (DEPTH, F, D), f32) * F ** -0.5,
        'ffn2_w_gu': nrm(ks[7], (DEPTH, D, 2 * F), f32) * D ** -0.5,
        'ffn2_w_down': nrm(ks[8], (DEPTH, F, D), f32) * F ** -0.5,
        'attn_w_qkv': nrm(ks[9], (N_ATTN_LAYERS, D, (ATTN_HEADS + 2 * ATTN_KV_HEADS) * ATTN_HEAD_DIM), f32) * D ** -0.5,
        'attn_w_o': nrm(ks[10], (N_ATTN_LAYERS, ATTN_HEADS * ATTN_HEAD_DIM, D), f32) * (ATTN_HEADS * ATTN_HEAD_DIM) ** -0.5,
        'attn_q_gain': 1.0 + 0.02 * nrm(ks[11], (N_ATTN_LAYERS, ATTN_HEAD_DIM), f32),
        'attn_k_gain': 1.0 + 0.02 * nrm(ks[12], (N_ATTN_LAYERS, ATTN_HEAD_DIM), f32),
        'attn_sinks': nrm(ks[13], (N_ATTN_LAYERS, ATTN_HEADS), f32),
        'hgrn_w_qfig': nrm(ks[14], (N_HGRN_LAYERS, D, 2 * HGRN_FORGET_DIM + 2 * D), f32) * D ** -0.5,
        'hgrn_w_o': nrm(ks[15], (N_HGRN_LAYERS, D, D), f32) * D ** -0.5,
        'hgrn_g_gain': 1.0 + 0.02 * nrm(ks[16], (N_HGRN_LAYERS, HGRN_HEAD_V), f32),
        'hgrn_lower_bounds': 0.5 * nrm(ks[17], (DEPTH, HGRN_FORGET_DIM), f32),
    }


def reference(x, c, ada_w, ada_b, norm_gains, ffn1_w_gu, ffn1_w_down, ffn2_w_gu, ffn2_w_down,
              attn_w_qkv, attn_w_o, attn_q_gain, attn_k_gain, attn_sinks,
              hgrn_w_qfig, hgrn_w_o, hgrn_g_gain, hgrn_lower_bounds):
    B = x.shape[0]
    s = jax.nn.softmax(hgrn_lower_bounds.astype(jnp.float32), axis=0)
    lb_all = jnp.cumsum(s, axis=0) - s[0]
    cond = jax.nn.silu(c)
    h = x
    for i in range(DEPTH):
        mod = (cond @ ada_w[i] + ada_b[i]).reshape(B, N_MOD, D_MODEL)
        sh1, sc1, g1, sh2, sc2, g2, sh3, sc3, g3 = [mod[:, j] for j in range(N_MOD)]
        u = modulate(rmsnorm(h, norm_gains[i, 0]), sh1, sc1)
        h = h + 0.5 * g1[:, None, :] * swiglu(u, ffn1_w_gu[i], ffn1_w_down[i])
        u = modulate(rmsnorm(h, norm_gains[i, 1]), sh2, sc2)
        j = i // N_MIXERS
        if i % N_MIXERS == 0:
            y = sliding_window_attention(u, attn_w_qkv[j], attn_w_o[j], attn_q_gain[j], attn_k_gain[j], attn_sinks[j])
        else:
            y = hgrn2(u, hgrn_w_qfig[j], hgrn_w_o[j], hgrn_g_gain[j], lb_all[i])
        h = h + g2[:, None, :] * y
        u = modulate(rmsnorm(h, norm_gains[i, 2]), sh3, sc3)
        h = h + 0.5 * g3[:, None, :] * swiglu(u, ffn2_w_gu[i], ffn2_w_down[i])
    return h
```

```python
import functools

import numpy as np
import jax
import jax.numpy as jnp
from jax import lax
from jax.experimental import pallas as pl
from jax.experimental.pallas import tpu as pltpu

F32 = jnp.float32
BF16 = jnp.bfloat16
EPS = 1e-6

WINDOW = 128
HEAD_DIM = 128
N_MOD = 9
HGRN_CHUNK = 128

V7X_VMEM_BYTES = 64 * 1024 * 1024
VMEM_HEADROOM_BYTES = 6 * 1024 * 1024
LANES = 128


def _vmem_limit(block_bytes):
    want = 2 * block_bytes + VMEM_HEADROOM_BYTES
    return int(min(max(want, 32 * 1024 * 1024), V7X_VMEM_BYTES - 4 * 1024 * 1024))


def _nbytes(shape, dtype):
    return int(np.prod(shape)) * jnp.dtype(dtype).itemsize


def _silu(x):
    return x * jax.nn.sigmoid(x)


def _dot(a, b):
    return jnp.dot(a, b, preferred_element_type=F32)


def _dot_nt(a, b):
    return lax.dot_general(a, b, (((1,), (1,)), ((), ())), preferred_element_type=F32)


def _dot_tn(a, b):
    return lax.dot_general(a, b, (((0,), (0,)), ((), ())), preferred_element_type=F32)


def _ada_kernel(c_ref, w_ref, b_ref, o_ref):
    cond = _silu(c_ref[...]).astype(BF16)
    o_ref[0] = _dot(cond, w_ref[0].astype(BF16)) + b_ref[0]


def _ada(c_pad, ada_w, ada_b, *, tn=512):
    depth, d, nd = ada_w.shape
    rows = c_pad.shape[0]
    blk = _nbytes((d, tn), F32) + _nbytes((rows, tn), F32) + _nbytes((rows, d), F32)
    return pl.pallas_call(
        _ada_kernel,
        grid=(depth, nd // tn),
        in_specs=[
            pl.BlockSpec((rows, d), lambda l, j: (0, 0)),
            pl.BlockSpec((1, d, tn), lambda l, j: (l, 0, j)),
            pl.BlockSpec((1, 1, tn), lambda l, j: (l, 0, j)),
        ],
        out_specs=pl.BlockSpec((1, rows, tn), lambda l, j: (l, 0, j)),
        out_shape=jax.ShapeDtypeStruct((depth, rows, nd), F32),
        compiler_params=pltpu.CompilerParams(
            dimension_semantics=("arbitrary", "arbitrary"),
            vmem_limit_bytes=_vmem_limit(blk)),
        name="ada_mod",
    )(c_pad, ada_w, ada_b.reshape(depth, 1, nd))


def _norm_mod_kernel(h_ref, gain_ref, shift_ref, scale_ref, o_ref):
    h = h_ref[...]
    y = h * lax.rsqrt(jnp.mean(h * h, axis=-1, keepdims=True) + EPS) * gain_ref[...]
    o_ref[...] = (y * (1.0 + scale_ref[0]) + shift_ref[0]).astype(o_ref.dtype)


def _norm_mod(h, gain, shift, scale, *, rows_per_batch, tm=512):
    m, d = h.shape
    tpb = rows_per_batch // tm
    blk = _nbytes((tm, d), F32) + _nbytes((tm, d), BF16)
    return pl.pallas_call(
        _norm_mod_kernel,
        grid=(m // tm,),
        in_specs=[
            pl.BlockSpec((tm, d), lambda i: (i, 0)),
            pl.BlockSpec((1, d), lambda i: (0, 0)),
            pl.BlockSpec((1, 1, d), lambda i: (i // tpb, 0, 0)),
            pl.BlockSpec((1, 1, d), lambda i: (i // tpb, 0, 0)),
        ],
        out_specs=pl.BlockSpec((tm, d), lambda i: (i, 0)),
        out_shape=jax.ShapeDtypeStruct((m, d), BF16),
        compiler_params=pltpu.CompilerParams(
            dimension_semantics=("arbitrary",),
            vmem_limit_bytes=_vmem_limit(blk)),
        name="norm_mod",
    )(h, gain.reshape(1, d), shift, scale)


def _mm_kernel(*refs, n_w, n_e, n_o, nk, epilogue):
    x_ref = refs[0]
    w_refs = refs[1:1 + n_w]
    e_refs = refs[1 + n_w:1 + n_w + n_e]
    o_refs = refs[1 + n_w + n_e:1 + n_w + n_e + n_o]
    acc_refs = refs[1 + n_w + n_e + n_o:]
    j = pl.program_id(1)

    def dots():
        x = x_ref[...]
        return [_dot(x, w[...]) for w in w_refs]

    if nk == 1:
        epilogue(dots(), j, e_refs, o_refs)
        return

    k = pl.program_id(2)

    @pl.when(k == 0)
    def _():
        for a, d in zip(acc_refs, dots()):
            a[...] = d

    if nk > 2:
        @pl.when((k > 0) & (k < nk - 1))
        def _():
            for a, d in zip(acc_refs, dots()):
                a[...] += d

    @pl.when(k == nk - 1)
    def _():
        epilogue([a[...] + d for a, d in zip(acc_refs, dots())], j, e_refs, o_refs)


def _mm(x, w, w_col_blocks, n_cols, extras, outs, epilogue, *, name, tm=1024, tn=1024, tk=4096):
    m, kdim = x.shape
    tk = min(tk, kdim)
    nk = kdim // tk
    grid = (m // tm, n_cols // tn, nk)
    in_specs = [pl.BlockSpec((tm, tk), lambda i, j, k: (i, k))]
    blk = _nbytes((tm, tk), x.dtype)
    for off in w_col_blocks:
        in_specs.append(pl.BlockSpec((tk, tn), lambda i, j, k, off=off: (k, j + off)))
        blk += _nbytes((tk, tn), w.dtype)
    for arr, bshape, imap in extras:
        in_specs.append(pl.BlockSpec(bshape, imap))
        blk += _nbytes(bshape, arr.dtype)
    out_specs = [pl.BlockSpec((tm, tn), lambda i, j, k: (i, j)) for _ in outs]
    out_shape = [jax.ShapeDtypeStruct((m, n_cols), dt) for dt in outs]
    blk += sum(_nbytes((tm, tn), dt) for dt in outs)
    n_w = len(w_col_blocks)
    scratch = [pltpu.VMEM((tm, tn), F32) for _ in range(n_w)] if nk > 1 else []
    scratch_bytes = (len(scratch) + 3 * n_w) * _nbytes((tm, tn), F32)
    kern = functools.partial(_mm_kernel, n_w=n_w, n_e=len(extras), n_o=len(outs), nk=nk,
                             epilogue=epilogue)
    res = pl.pallas_call(
        kern,
        grid=grid,
        in_specs=in_specs,
        out_specs=out_specs,
        out_shape=out_shape,
        scratch_shapes=scratch,
        compiler_params=pltpu.CompilerParams(
            dimension_semantics=("parallel", "parallel", "arbitrary"),
            vmem_limit_bytes=_vmem_limit(blk + scratch_bytes // 2)),
        name=name,
    )(x, *([w] * n_w), *[e[0] for e in extras])
    return res


def _ep_swiglu(accs, j, e_refs, o_refs):
    a, b = accs
    o_refs[0][...] = (_silu(a) * b).astype(o_refs[0].dtype)


def _ep_residual(accs, j, e_refs, o_refs, *, coef):
    h_ref, gate_ref = e_refs
    o_refs[0][...] = h_ref[...] + (coef * gate_ref[0]) * accs[0]


def _ep_qkv(accs, j, e_refs, o_refs, *, n_norm_tiles):
    y = accs[0]
    gain_ref, = e_refs
    o_ref = o_refs[0]
    tn = y.shape[1]

    @pl.when(j < n_norm_tiles)
    def _():
        g = gain_ref[0]
        for hd in range(tn // HEAD_DIM):
            yh = y[:, hd * HEAD_DIM:(hd + 1) * HEAD_DIM]
            inv = lax.rsqrt(jnp.mean(yh * yh, axis=-1, keepdims=True) + EPS)
            o_ref[:, hd * HEAD_DIM:(hd + 1) * HEAD_DIM] = (yh * inv * g).astype(o_ref.dtype)

    @pl.when(j >= n_norm_tiles)
    def _():
        o_ref[...] = y.astype(o_ref.dtype)


def _ep_silu(accs, j, e_refs, o_refs):
    o_refs[0][...] = _silu(accs[0]).astype(o_refs[0].dtype)


def _ep_cast(accs, j, e_refs, o_refs):
    o_refs[0][...] = accs[0].astype(o_refs[0].dtype)


def _ep_forget(accs, j, e_refs, o_refs, *, layer):
    f = accs[0]
    lbp = e_refs[0][...]
    depth = lbp.shape[0]
    mx = lbp[0:1]
    for l in range(1, depth):
        mx = jnp.maximum(mx, lbp[l:l + 1])
    es = [jnp.exp(lbp[l:l + 1] - mx) for l in range(depth)]
    tot = es[0]
    for l in range(1, depth):
        tot = tot + es[l]
    s = [e / tot for e in es]
    cum = s[0]
    for l in range(1, layer + 1):
        cum = cum + s[l]
    lb = cum - s[0]
    log_sig = jnp.minimum(f, 0.0) - jnp.log1p(jnp.exp(-jnp.abs(f)))
    a = jnp.log(lb)
    b = jnp.log1p(-lb) + log_sig
    hi = jnp.maximum(a, b)
    lo = jnp.minimum(a, b)
    log_f = hi + jnp.log1p(jnp.exp(lo - hi))
    o_refs[0][...] = log_f
    o_refs[1][...] = (1.0 - lb) * jax.nn.sigmoid(-f)


def _attn_kernel(slope_ref, sink_ref, q_ref, kc_ref, kp_ref, vc_ref, vp_ref, o_ref, *, n_kv, group):
    n = pl.program_id(1)
    blk = q_ref.shape[0]
    qi = lax.broadcasted_iota(jnp.int32, (blk, blk), 0)
    kj = lax.broadcasted_iota(jnp.int32, (blk, blk), 1)
    dist_c = (qi - kj).astype(F32)
    dist_p = dist_c + float(blk)
    valid_c = kj <= qi
    valid_p = (kj > qi) & (n > 0)
    neg = -jnp.inf

    def kv_head(hk, carry):
        ko = pl.multiple_of(hk * HEAD_DIM, HEAD_DIM)
        kc = kc_ref[:, pl.ds(ko, HEAD_DIM)]
        kp = kp_ref[:, pl.ds(ko, HEAD_DIM)]
        vc = vc_ref[:, pl.ds(ko, HEAD_DIM)]
        vp = vp_ref[:, pl.ds(ko, HEAD_DIM)]
        for g in range(group):
            head = hk * group + g
            qo = pl.multiple_of(head * HEAD_DIM, HEAD_DIM)
            q = q_ref[:, pl.ds(qo, HEAD_DIM)]
            slope = slope_ref[head]
            sink = sink_ref[head]
            lc = jnp.where(valid_c, _dot_nt(q, kc) - slope * dist_c, neg)
            lp = jnp.where(valid_p, _dot_nt(q, kp) - slope * dist_p, neg)
            m = jnp.maximum(jnp.maximum(jnp.max(lc, axis=-1, keepdims=True),
                                        jnp.max(lp, axis=-1, keepdims=True)), sink)
            pc = jnp.exp(lc - m)
            pp = jnp.exp(lp - m)
            denom = (jnp.sum(pc, axis=-1, keepdims=True) + jnp.sum(pp, axis=-1, keepdims=True)
                     + jnp.exp(sink - m))
            out = _dot(pc.astype(BF16), vc) + _dot(pp.astype(BF16), vp)
            o_ref[:, pl.ds(qo, HEAD_DIM)] = (out / denom).astype(o_ref.dtype)
        return carry

    lax.fori_loop(0, n_kv, kv_head, 0)


def _attention(qkv, slopes, sinks, *, batch, seq, n_heads, n_kv):
    m = qkv.shape[0]
    blk = WINDOW
    nb = seq // blk
    dq = n_heads * HEAD_DIM
    dkv = n_kv * HEAD_DIM
    k_col = dq // dkv
    v_col = k_col + 1
    cur = lambda b, n: b * nb + n
    prev = lambda b, n: b * nb + jnp.maximum(n - 1, 0)
    smem = pl.BlockSpec(memory_space=pltpu.SMEM)
    blk_bytes = 2 * _nbytes((blk, dq), BF16) + 4 * _nbytes((blk, dkv), BF16)
    return pl.pallas_call(
        functools.partial(_attn_kernel, n_kv=n_kv, group=n_heads // n_kv),
        grid=(batch, nb),
        in_specs=[
            smem, smem,
            pl.BlockSpec((blk, dq), lambda b, n: (cur(b, n), 0)),
            pl.BlockSpec((blk, dkv), lambda b, n: (cur(b, n), k_col)),
            pl.BlockSpec((blk, dkv), lambda b, n: (prev(b, n), k_col)),
            pl.BlockSpec((blk, dkv), lambda b, n: (cur(b, n), v_col)),
            pl.BlockSpec((blk, dkv), lambda b, n: (prev(b, n), v_col)),
        ],
        out_specs=pl.BlockSpec((blk, dq), lambda b, n: (cur(b, n), 0)),
        out_shape=jax.ShapeDtypeStruct((m, dq), BF16),
        compiler_params=pltpu.CompilerParams(
            dimension_semantics=("arbitrary", "arbitrary"),
            vmem_limit_bytes=_vmem_limit(blk_bytes)),
        name="swa_attention",
    )(slopes, sinks, qkv, qkv, qkv, qkv, qkv)


def _hgrn_tables(c):
    levels = int(np.log2(c))
    r = np.arange(c)[:, None]
    j = np.arange(c)[None, :]
    blocks = [(j <= r), (j > r)]
    masks = []
    for lv in range(levels):
        half = c >> (lv + 1)
        base = (r // (2 * half)) * (2 * half)
        ref = base + half - 1
        odd = r > ref
        blocks.append(np.where(odd, (j > ref) & (j <= r), (j > r) & (j <= ref)))
        t, s = r, j
        same = (t // (2 * half)) == (s // (2 * half))
        masks.append(same & ((t % (2 * half)) >= half) & ((s % (2 * half)) < half))
    masks.append(r == j)
    w = np.concatenate(blocks, axis=0).astype(np.float32)
    w3 = np.concatenate([w, w, w], axis=1)
    return jnp.asarray(w3, BF16), jnp.asarray(np.stack(masks).astype(np.float32)), levels


def _hgrn_kernel(w3_ref, mask_ref, gain_ref, q_ref, k_ref, v_ref, lf_ref, gate_ref, o_ref, s_ref,
                 *, c, levels):
    @pl.when(pl.program_id(2) == 0)
    def _():
        s_ref[...] = jnp.zeros_like(s_ref)

    n_chunks = q_ref.shape[0] // c
    ones = jnp.ones((3 * c, LANES), BF16)

    def chunk(ci, carry):
        rows = pl.ds(pl.multiple_of(ci * c, c), c)
        lf = lf_ref[rows, :]
        q = q_ref[rows, :]
        k = k_ref[rows, :]
        vb = v_ref[rows, :].astype(BF16)
        hi = lf.astype(BF16)
        r1 = lf - hi.astype(F32)
        mid = r1.astype(BF16)
        lo = (r1 - mid.astype(F32)).astype(BF16)
        lf3 = jnp.concatenate([hi, mid, lo], axis=0)
        e = jnp.exp(_dot(w3_ref[...], lf3))
        e_b = e[0:c]
        e_bl = e[c:2 * c]
        s_old = s_ref[...]
        inter = _dot((q * e_b).astype(BF16), s_old.astype(BF16))
        a = _dot_nt(q.astype(BF16), k.astype(BF16)) * mask_ref[levels]
        for lv in range(levels):
            el = e[(2 + lv) * c:(3 + lv) * c]
            a = a + _dot_nt((q * el).astype(BF16), (k * el).astype(BF16)) * mask_ref[lv]
        o = inter + _dot(a.astype(BF16), vb)
        decay = jnp.exp(_dot_tn(lf3, ones))
        s_ref[...] = decay * s_old + _dot_tn((k * e_bl).astype(BF16), vb)
        y = o * lax.rsqrt(jnp.mean(o * o, axis=-1, keepdims=True) + EPS) * gain_ref[...]
        o_ref[rows, :] = (y * gate_ref[rows, :]).astype(o_ref.dtype)
        return carry

    lax.fori_loop(0, n_chunks, chunk, 0)


def _hgrn_recurrence(q, k, v, log_f, gate, gain, *, batch, seq, tb=1024):
    m, d = q.shape
    n_heads = d // HEAD_DIM
    c = HGRN_CHUNK
    w3, masks, levels = _hgrn_tables(c)
    nt = seq // tb
    tile = pl.BlockSpec((tb, HEAD_DIM), lambda b, h, t: (b * nt + t, h))
    const2 = lambda b, h, t: (0, 0)
    blk = (6 * _nbytes((tb, HEAD_DIM), F32) + _nbytes(w3.shape, BF16) + _nbytes(masks.shape, F32))
    return pl.pallas_call(
        functools.partial(_hgrn_kernel, c=c, levels=levels),
        grid=(batch, n_heads, nt),
        in_specs=[
            pl.BlockSpec(w3.shape, const2),
            pl.BlockSpec(masks.shape, lambda b, h, t: (0, 0, 0)),
            pl.BlockSpec((1, HEAD_DIM), const2),
            tile, tile, tile, tile, tile,
        ],
        out_specs=tile,
        out_shape=jax.ShapeDtypeStruct((m, d), BF16),
        scratch_shapes=[pltpu.VMEM((HEAD_DIM, HEAD_DIM), F32)],
        compiler_params=pltpu.CompilerParams(
            dimension_semantics=("arbitrary", "arbitrary", "arbitrary"),
            vmem_limit_bytes=_vmem_limit(blk)),
        name="hgrn2_recurrence",
    )(w3, masks, gain.reshape(1, HEAD_DIM), q, k, v, log_f, gate)


def _bcast_mod(mod_l, idx, batch):
    d = mod_l.shape[-1] // N_MOD
    return mod_l[:batch, idx * d:(idx + 1) * d].reshape(batch, 1, d)


def _ffn(h, u, w_gu, w_down, gate, *, rows_per_batch):
    d, f2 = w_gu.shape
    f = f2 // 2
    tm, tn = 1024, 512
    act, = _mm(u, w_gu.astype(BF16), [0, f // tn], f, [], [BF16], _ep_swiglu,
               name="ffn_gate_up", tm=tm, tn=tn)
    return _residual_mm(act, w_down, h, gate, 0.5, rows_per_batch=rows_per_batch, name="ffn_down")


def _residual_mm(x, w, h, gate, coef, *, rows_per_batch, name):
    tm, tn = 1024, 1024
    tpb = rows_per_batch // tm
    extras = [
        (h, (tm, tn), lambda i, j, k: (i, j)),
        (gate, (1, 1, tn), lambda i, j, k: (i // tpb, 0, j)),
    ]
    out, = _mm(x, w.astype(BF16), [0], w.shape[1], extras, [F32],
               functools.partial(_ep_residual, coef=coef), name=name, tm=tm, tn=tn, tk=2048)
    return out


def kernel(x, c, ada_w, ada_b, norm_gains, ffn1_w_gu, ffn1_w_down, ffn2_w_gu, ffn2_w_down,
           attn_w_qkv, attn_w_o, attn_q_gain, attn_k_gain, attn_sinks,
           hgrn_w_qfig, hgrn_w_o, hgrn_g_gain, hgrn_lower_bounds):
    batch, seq, d = x.shape
    depth = ada_w.shape[0]
    m = batch * seq
    n_mixers = 2

    c_pad = jnp.zeros((8, d), F32).at[:batch].set(c)
    mod = _ada(c_pad, ada_w, ada_b)

    h = x.reshape(m, d)
    for i in range(depth):
        md = lambda idx: _bcast_mod(mod[i], idx, batch)
        u = _norm_mod(h, norm_gains[i, 0], md(0), md(1), rows_per_batch=seq)
        h = _ffn(h, u, ffn1_w_gu[i], ffn1_w_down[i], md(2), rows_per_batch=seq)
        u = _norm_mod(h, norm_gains[i, 1], md(3), md(4), rows_per_batch=seq)
        jm = i // n_mixers
        if i % n_mixers == 0:
            n_heads = attn_sinks.shape[1]
            w_qkv = attn_w_qkv[jm].astype(BF16)
            n_kv = (w_qkv.shape[1] // HEAD_DIM - n_heads) // 2
            tn = n_kv * HEAD_DIM
            q_tiles = n_heads // n_kv
            ones = jnp.ones((1, HEAD_DIM), F32)
            gains = jnp.stack([attn_q_gain[jm].reshape(1, HEAD_DIM) * HEAD_DIM ** -0.5] * q_tiles
                              + [attn_k_gain[jm].reshape(1, HEAD_DIM), ones])
            qkv, = _mm(u, w_qkv, [0], w_qkv.shape[1],
                       [(gains, (1, 1, HEAD_DIM), lambda i_, j_, k_: (j_, 0, 0))], [BF16],
                       functools.partial(_ep_qkv, n_norm_tiles=q_tiles + 1),
                       name="attn_qkv", tm=1024, tn=tn)
            heads = jnp.arange(1, n_heads + 1, dtype=F32)
            slopes = jnp.exp2(-8.0 * heads / n_heads)
            y = _attention(qkv, slopes, attn_sinks[jm].astype(F32), batch=batch, seq=seq,
                           n_heads=n_heads, n_kv=n_kv)
            h = _residual_mm(y, attn_w_o[jm], h, md(5), 1.0, rows_per_batch=seq, name="attn_out")
        else:
            w = hgrn_w_qfig[jm].astype(BF16)
            fd = (w.shape[1] - 2 * d) // 2
            tn = 1024
            qh, = _mm(u, w, [0], fd, [], [F32], _ep_silu, name="hgrn_q", tn=tn)
            lbp = hgrn_lower_bounds.astype(F32)
            log_f, kh = _mm(u, w, [fd // tn], fd,
                            [(lbp, (depth, tn), lambda i_, j_, k_: (0, j_))], [F32, F32],
                            functools.partial(_ep_forget, layer=i), name="hgrn_f", tn=tn)
            vh, = _mm(u, w, [2 * fd // tn], d, [], [F32], _ep_cast, name="hgrn_i", tn=tn)
            gh, = _mm(u, w, [(2 * fd + d) // tn], d, [], [F32], _ep_silu, name="hgrn_g", tn=tn)
            y = _hgrn_recurrence(qh, kh, vh, log_f, gh, hgrn_g_gain[jm], batch=batch, seq=seq)
            h = _residual_mm(y, hgrn_w_o[jm], h, md(5), 1.0, rows_per_batch=seq, name="hgrn_out")
        u = _norm_mod(h, norm_gains[i, 2], md(6), md(7), rows_per_batch=seq)
        h = _ffn(h, u, ffn2_w_gu[i], ffn2_w_down[i], md(8), rows_per_batch=seq)
    return h.reshape(batch, seq, d)
```

```python
import functools

import numpy as np
import jax
import jax.numpy as jnp
from jax import lax
from jax.experimental import pallas as pl
from jax.experimental.pallas import tpu as pltpu

F32 = jnp.float32
BF16 = jnp.bfloat16
EPS = 1e-6
LOG2_E = 1.4426950408889634

WINDOW = 128
HEAD_DIM = 128
N_MOD = 9
HGRN_CHUNK = 128

V7X_VMEM_BYTES = 64 * 1024 * 1024
VMEM_HEADROOM_BYTES = 6 * 1024 * 1024
LANES = 128


def _vmem_limit(block_bytes):
    want = 2 * block_bytes + VMEM_HEADROOM_BYTES
    return int(min(max(want, 32 * 1024 * 1024), V7X_VMEM_BYTES - 4 * 1024 * 1024))


def _nbytes(shape, dtype):
    return int(np.prod(shape)) * jnp.dtype(dtype).itemsize


def _silu(x):
    return x * jax.nn.sigmoid(x)


def _dot(a, b):
    return jnp.dot(a, b, preferred_element_type=F32)


def _dot_nt(a, b):
    return lax.dot_general(a, b, (((1,), (1,)), ((), ())), preferred_element_type=F32)


def _dot_tn(a, b):
    return lax.dot_general(a, b, (((0,), (0,)), ((), ())), preferred_element_type=F32)


def _ada_kernel(c_ref, w_ref, b_ref, o_ref):
    cond = _silu(c_ref[...]).astype(BF16)
    o_ref[0] = _dot(cond, w_ref[0].astype(BF16)) + b_ref[0]


def _ada(c_pad, ada_w, ada_b, *, tn=512):
    depth, d, nd = ada_w.shape
    rows = c_pad.shape[0]
    blk = _nbytes((d, tn), F32) + _nbytes((rows, tn), F32) + _nbytes((rows, d), F32)
    return pl.pallas_call(
        _ada_kernel,
        grid=(depth, nd // tn),
        in_specs=[
            pl.BlockSpec((rows, d), lambda l, j: (0, 0)),
            pl.BlockSpec((1, d, tn), lambda l, j: (l, 0, j)),
            pl.BlockSpec((1, 1, tn), lambda l, j: (l, 0, j)),
        ],
        out_specs=pl.BlockSpec((1, rows, tn), lambda l, j: (l, 0, j)),
        out_shape=jax.ShapeDtypeStruct((depth, rows, nd), F32),
        compiler_params=pltpu.CompilerParams(
            dimension_semantics=("arbitrary", "arbitrary"),
            vmem_limit_bytes=_vmem_limit(blk)),
        name="ada_mod",
    )(c_pad, ada_w, ada_b.reshape(depth, 1, nd))


def _norm_mod_kernel(h_ref, gain_ref, shift_ref, scale_ref, o_ref):
    h = h_ref[...]
    y = h * lax.rsqrt(jnp.mean(h * h, axis=-1, keepdims=True) + EPS) * gain_ref[...]
    o_ref[...] = (y * (1.0 + scale_ref[0]) + shift_ref[0]).astype(o_ref.dtype)


def _norm_mod(h, gain, shift, scale, *, rows_per_batch, tm=512):
    m, d = h.shape
    tpb = rows_per_batch // tm
    blk = _nbytes((tm, d), F32) + _nbytes((tm, d), BF16)
    return pl.pallas_call(
        _norm_mod_kernel,
        grid=(m // tm,),
        in_specs=[
            pl.BlockSpec((tm, d), lambda i: (i, 0)),
            pl.BlockSpec((1, d), lambda i: (0, 0)),
            pl.BlockSpec((1, 1, d), lambda i: (i // tpb, 0, 0)),
            pl.BlockSpec((1, 1, d), lambda i: (i // tpb, 0, 0)),
        ],
        out_specs=pl.BlockSpec((tm, d), lambda i: (i, 0)),
        out_shape=jax.ShapeDtypeStruct((m, d), BF16),
        compiler_params=pltpu.CompilerParams(
            dimension_semantics=("arbitrary",),
            vmem_limit_bytes=_vmem_limit(blk)),
        name="norm_mod",
    )(h, gain.reshape(1, d), shift, scale)


def _mm_kernel(*refs, n_w, n_e, n_o, nk, epilogue):
    x_ref = refs[0]
    w_refs = refs[1:1 + n_w]
    e_refs = refs[1 + n_w:1 + n_w + n_e]
    o_refs = refs[1 + n_w + n_e:1 + n_w + n_e + n_o]
    acc_refs = refs[1 + n_w + n_e + n_o:]
    j = pl.program_id(1)

    def dots():
        x = x_ref[...]
        return [_dot(x, w[...]) for w in w_refs]

    if nk == 1:
        epilogue(dots(), j, e_refs, o_refs)
        return

    k = pl.program_id(2)

    @pl.when(k == 0)
    def _():
        for a, d in zip(acc_refs, dots()):
            a[...] = d

    if nk > 2:
        @pl.when((k > 0) & (k < nk - 1))
        def _():
            for a, d in zip(acc_refs, dots()):
                a[...] += d

    @pl.when(k == nk - 1)
    def _():
        epilogue([a[...] + d for a, d in zip(acc_refs, dots())], j, e_refs, o_refs)


def _mm(x, w, w_col_blocks, n_cols, extras, outs, epilogue, *, name, tm=1024, tn=1024, tk=4096):
    m, kdim = x.shape
    tk = min(tk, kdim)
    nk = kdim // tk
    grid = (m // tm, n_cols // tn, nk)
    in_specs = [pl.BlockSpec((tm, tk), lambda i, j, k: (i, k))]
    blk = _nbytes((tm, tk), x.dtype)
    for off in w_col_blocks:
        in_specs.append(pl.BlockSpec((tk, tn), lambda i, j, k, off=off: (k, j + off)))
        blk += _nbytes((tk, tn), w.dtype)
    for arr, bshape, imap in extras:
        in_specs.append(pl.BlockSpec(bshape, imap))
        blk += _nbytes(bshape, arr.dtype)
    out_specs = [pl.BlockSpec((tm, tn), lambda i, j, k: (i, j)) for _ in outs]
    out_shape = [jax.ShapeDtypeStruct((m, n_cols), dt) for dt in outs]
    blk += sum(_nbytes((tm, tn), dt) for dt in outs)
    n_w = len(w_col_blocks)
    scratch = [pltpu.VMEM((tm, tn), F32) for _ in range(n_w)] if nk > 1 else []
    scratch_bytes = (len(scratch) + 3 * n_w) * _nbytes((tm, tn), F32)
    kern = functools.partial(_mm_kernel, n_w=n_w, n_e=len(extras), n_o=len(outs), nk=nk,
                             epilogue=epilogue)
    res = pl.pallas_call(
        kern,
        grid=grid,
        in_specs=in_specs,
        out_specs=out_specs,
        out_shape=out_shape,
        scratch_shapes=scratch,
        compiler_params=pltpu.CompilerParams(
            dimension_semantics=("parallel", "parallel", "arbitrary"),
            vmem_limit_bytes=_vmem_limit(blk + scratch_bytes // 2)),
        name=name,
    )(x, *([w] * n_w), *[e[0] for e in extras])
    return res


def _ep_swiglu(accs, j, e_refs, o_refs):
    a, b = accs
    o_refs[0][...] = (_silu(a) * b).astype(o_refs[0].dtype)


def _ep_residual(accs, j, e_refs, o_refs, *, coef):
    h_ref, gate_ref = e_refs
    o_refs[0][...] = h_ref[...] + (coef * gate_ref[0]) * accs[0]


def _ep_qkv(accs, j, e_refs, o_refs, *, n_norm_tiles):
    y = accs[0]
    gain_ref, = e_refs
    o_ref = o_refs[0]
    tn = y.shape[1]

    @pl.when(j < n_norm_tiles)
    def _():
        g = gain_ref[0]
        for hd in range(tn // HEAD_DIM):
            yh = y[:, hd * HEAD_DIM:(hd + 1) * HEAD_DIM]
            inv = lax.rsqrt(jnp.mean(yh * yh, axis=-1, keepdims=True) + EPS)
            o_ref[:, hd * HEAD_DIM:(hd + 1) * HEAD_DIM] = (yh * inv * g).astype(o_ref.dtype)

    @pl.when(j >= n_norm_tiles)
    def _():
        o_ref[...] = y.astype(o_ref.dtype)


def _ep_silu(accs, j, e_refs, o_refs):
    o_refs[0][...] = _silu(accs[0]).astype(o_refs[0].dtype)


def _ep_cast(accs, j, e_refs, o_refs):
    o_refs[0][...] = accs[0].astype(o_refs[0].dtype)


def _ep_forget(accs, j, e_refs, o_refs, *, layer):
    f = accs[0]
    lbp = e_refs[0][...]
    depth = lbp.shape[0]
    mx = lbp[0:1]
    for l in range(1, depth):
        mx = jnp.maximum(mx, lbp[l:l + 1])
    es = [jnp.exp(lbp[l:l + 1] - mx) for l in range(depth)]
    tot = es[0]
    for l in range(1, depth):
        tot = tot + es[l]
    s = [e / tot for e in es]
    cum = s[0]
    for l in range(1, layer + 1):
        cum = cum + s[l]
    lb = cum - s[0]
    log_sig = jnp.minimum(f, 0.0) - jnp.log1p(jnp.exp(-jnp.abs(f)))
    a = jnp.log(lb)
    b = jnp.log1p(-lb) + log_sig
    hi = jnp.maximum(a, b)
    lo = jnp.minimum(a, b)
    log_f = hi + jnp.log1p(jnp.exp(lo - hi))
    o_refs[0][...] = log_f * LOG2_E
    o_refs[1][...] = ((1.0 - lb) * jax.nn.sigmoid(-f)).astype(o_refs[1].dtype)


def _split3(x):
    hi = x.astype(BF16)
    r = x - hi.astype(F32)
    mid = r.astype(BF16)
    lo = (r - mid.astype(F32)).astype(BF16)
    return hi, mid, lo


def _attn_tables(sinks, *, n_kv, group, blk):
    n_heads = n_kv * group
    q_feat = jnp.zeros((n_heads, blk, LANES), BF16)
    for i, s_i in enumerate(_split3(sinks)):
        q_feat = q_feat.at[:, :, i].set(jnp.broadcast_to(s_i[:, None], (n_heads, blk)))
    q_feat = q_feat.reshape(n_kv, group * blk, LANES)
    k_feat = np.zeros((2 * blk, LANES), np.float32)
    k_feat[0, 0:3] = 1.0
    qi = np.arange(blk)[:, None]
    kj = np.arange(2 * blk)[None, :]
    dist = (qi + blk - kj).astype(np.float32)
    valid = (dist >= 0) & (dist < WINDOW)
    sink_col = kj == 0
    inf = np.float32(np.inf)
    dist_all = np.where(sink_col, np.float32(0), np.where(valid, dist, inf))
    dist_first = np.where(sink_col, np.float32(0), np.where(valid & (kj >= blk), dist, inf))
    return q_feat, jnp.asarray(k_feat, BF16), jnp.asarray(dist_all), jnp.asarray(dist_first)


def _attn_kernel(slope_ref, qf_ref, kf_ref, dist_ref, dist_first_ref, q_ref, kc_ref, kp_ref, vc_ref,
                 vp_ref, o_ref, d_scr, *, n_kv, group):
    blk = q_ref.shape[0]
    d_scr[...] = jnp.where(pl.program_id(1) > 0, dist_ref[...], dist_first_ref[...])
    keep = (lax.broadcasted_iota(jnp.int32, (16, HEAD_DIM), 0) != 0).astype(BF16)
    ones = jnp.ones((2 * blk, HEAD_DIM), BF16)
    k_feat = kf_ref[...]

    def kv_head(hk, carry):
        ko = pl.ds(pl.multiple_of(hk * HEAD_DIM, HEAD_DIM), HEAD_DIM)
        kp = kp_ref[:, ko]
        vp = vp_ref[:, ko]
        vp = jnp.concatenate([vp[:16] * keep, vp[16:]], axis=0)
        kp = jnp.concatenate([kp[:16] * keep, kp[16:]], axis=0)
        k2 = jnp.concatenate([jnp.concatenate([kp, kc_ref[:, ko]], axis=0), k_feat], axis=1)
        v2 = jnp.concatenate([jnp.concatenate([vp, vc_ref[:, ko]], axis=0), ones], axis=1)
        cols = [pl.ds(pl.multiple_of((hk * group + g) * HEAD_DIM, HEAD_DIM), HEAD_DIM)
                for g in range(group)]
        q4 = jnp.concatenate([q_ref[:, cg] for cg in cols], axis=0)
        q4 = jnp.concatenate([q4, qf_ref[hk]], axis=1)
        s = _dot_nt(q4, k2)
        dist = d_scr[...]
        logits = jnp.concatenate(
            [s[g * blk:(g + 1) * blk] - slope_ref[hk * group + g] * dist for g in range(group)],
            axis=0)
        m = jnp.max(logits, axis=-1, keepdims=True)
        p = jnp.exp(logits - m).astype(BF16)
        o = _dot(p, v2)
        out = o[:, :HEAD_DIM] / o[:, HEAD_DIM:]
        for g, cg in enumerate(cols):
            o_ref[:, cg] = out[g * blk:(g + 1) * blk].astype(o_ref.dtype)
        return carry

    lax.fori_loop(0, n_kv, kv_head, 0, unroll=2)


def _attention(qkv, slopes, sinks, *, batch, seq, n_heads, n_kv):
    m = qkv.shape[0]
    blk = WINDOW
    group = n_heads // n_kv
    nb = seq // blk
    dq = n_heads * HEAD_DIM
    dkv = n_kv * HEAD_DIM
    k_col = dq // dkv
    v_col = k_col + 1
    q_feat, k_feat, dist, dist_first = _attn_tables(sinks, n_kv=n_kv, group=group, blk=blk)
    cur = lambda b, n: b * nb + n
    prev = lambda b, n: b * nb + jnp.maximum(n - 1, 0)
    const2 = lambda b, n: (0, 0)
    blk_bytes = (2 * _nbytes((blk, dq), BF16) + 4 * _nbytes((blk, dkv), BF16)
                 + _nbytes(q_feat.shape, BF16) + 3 * _nbytes(dist.shape, F32))
    return pl.pallas_call(
        functools.partial(_attn_kernel, n_kv=n_kv, group=group),
        grid=(batch, nb),
        in_specs=[
            pl.BlockSpec(memory_space=pltpu.SMEM),
            pl.BlockSpec(q_feat.shape, lambda b, n: (0, 0, 0)),
            pl.BlockSpec(k_feat.shape, const2),
            pl.BlockSpec(dist.shape, const2),
            pl.BlockSpec(dist.shape, const2),
            pl.BlockSpec((blk, dq), lambda b, n: (cur(b, n), 0)),
            pl.BlockSpec((blk, dkv), lambda b, n: (cur(b, n), k_col)),
            pl.BlockSpec((blk, dkv), lambda b, n: (prev(b, n), k_col)),
            pl.BlockSpec((blk, dkv), lambda b, n: (cur(b, n), v_col)),
            pl.BlockSpec((blk, dkv), lambda b, n: (prev(b, n), v_col)),
        ],
        out_specs=pl.BlockSpec((blk, dq), lambda b, n: (cur(b, n), 0)),
        out_shape=jax.ShapeDtypeStruct((m, dq), BF16),
        scratch_shapes=[pltpu.VMEM(dist.shape, F32)],
        compiler_params=pltpu.CompilerParams(
            dimension_semantics=("arbitrary", "arbitrary"),
            vmem_limit_bytes=_vmem_limit(blk_bytes)),
        name="swa_attention",
    )(slopes, q_feat, k_feat, dist, dist_first, qkv, qkv, qkv, qkv, qkv)


def _hgrn_tables(c):
    halves = [c >> (lv + 1) for lv in range(int(np.log2(c)))]
    r = np.arange(c)[:, None]
    j = np.arange(c)[None, :]

    def partial_sums(half):
        ref = (r // (2 * half)) * (2 * half) + half - 1
        return np.where(r > ref, (j > ref) & (j <= r), (j > r) & (j <= ref))

    masks = [((r // (2 * h)) == (j // (2 * h))) & ((r % (2 * h)) >= h) & ((j % (2 * h)) < h)
             for h in halves]
    masks.append(r == j)
    w = np.concatenate([j <= r, partial_sums(4), partial_sums(2)], axis=0).astype(np.float32)
    w3 = np.concatenate([w, w, w], axis=1)
    return jnp.asarray(w3, BF16), jnp.asarray(np.stack(masks).astype(np.float32), BF16), halves


def _hgrn_kernel(w3_ref, mask_ref, gain_ref, q_ref, k_ref, v_ref, lf_ref, gate_ref, o_ref,
                 s_ref, b_scr, *, c, halves):
    @pl.when(pl.program_id(2) == 0)
    def _():
        s_ref[...] = jnp.zeros_like(s_ref)

    n_chunks = q_ref.shape[0] // c
    zero_h = jnp.zeros((c, LANES), BF16)
    n_lv = len(halves)

    def block_diag(x0, x1):
        return jnp.concatenate([jnp.concatenate([x0, zero_h], axis=1),
                                jnp.concatenate([zero_h, x1], axis=1)], axis=0)

    def both(x):
        return x[:, :LANES], x[:, LANES:]

    def pair_mask(i):
        mk = mask_ref[i]
        return jnp.concatenate([mk, mk], axis=1)

    def chunk(ci, carry):
        rows = pl.ds(pl.multiple_of(ci * c, c), c)
        lf = lf_ref[rows, :]
        q = q_ref[rows, :]
        k = k_ref[rows, :]
        v = v_ref[rows, :]
        lf3 = jnp.concatenate(_split3(lf), axis=0)
        d = _dot(w3_ref[...], lf3)
        b = d[0:c]
        b_scr[...] = b

        def level_exponent(half):
            pieces = []
            for g0 in range(0, c, 2 * half):
                ref = jnp.broadcast_to(b_scr[g0 + half - 1:g0 + half, :], (half, 2 * LANES))
                pieces += [ref - b[g0:g0 + half], b[g0 + half:g0 + 2 * half] - ref]
            return jnp.concatenate(pieces, axis=0)

        k_bd = block_diag(*both(k))
        e1 = jnp.exp2(lf).astype(BF16)
        p = _dot_nt(jnp.concatenate([q, q * e1], axis=0), k_bd)
        a = p[0:c].astype(BF16) * pair_mask(n_lv) + p[c:2 * c].astype(BF16) * pair_mask(n_lv - 1)
        for li, half in enumerate(halves[:-1]):
            if half >= 8:
                dn = level_exponent(half)
            else:
                dn = d[c:2 * c] if half == 4 else d[2 * c:3 * c]
            e = jnp.exp2(dn).astype(BF16)
            a = a + _dot_nt(q * e, block_diag(*both(k * e))).astype(BF16) * pair_mask(li)
        intra = _dot(a, block_diag(*both(v)))

        s0 = s_ref[0]
        s1 = s_ref[1]
        inter = _dot_nt(q * jnp.exp2(b).astype(BF16), block_diag(s0.astype(BF16), s1.astype(BF16)))
        o = inter + intra

        b_last = b_scr[c - 1:c, :]
        u = _dot_tn(v, k * jnp.exp2(b_last - b).astype(BF16))
        decay = jnp.exp2(b_last)
        s_ref[0] = s0 * decay[:, :LANES] + u[:LANES, :LANES]
        s_ref[1] = s1 * decay[:, LANES:] + u[LANES:, LANES:]

        gate = gate_ref[rows, :].astype(F32)
        gain = gain_ref[...]
        for hh in range(2):
            lanes = slice(hh * LANES, (hh + 1) * LANES)
            oh = o[:, lanes]
            y = oh * lax.rsqrt(jnp.mean(oh * oh, axis=-1, keepdims=True) + EPS) * gain
            o_ref[rows, lanes] = (y * gate[:, lanes]).astype(o_ref.dtype)
        return carry

    lax.fori_loop(0, n_chunks, chunk, 0, unroll=2)


def _hgrn_recurrence(q, k, v, log_f, gate, gain, *, batch, seq, tb=1024):
    m, d = q.shape
    c = HGRN_CHUNK
    w3, masks, halves = _hgrn_tables(c)
    nt = seq // tb
    width = 2 * HEAD_DIM
    tile = pl.BlockSpec((tb, width), lambda b, h, t: (b * nt + t, h))
    const2 = lambda b, h, t: (0, 0)
    blk = (_nbytes((tb, width), F32) + 5 * _nbytes((tb, width), BF16)
           + _nbytes(w3.shape, BF16) + _nbytes(masks.shape, BF16))
    return pl.pallas_call(
        functools.partial(_hgrn_kernel, c=c, halves=halves),
        grid=(batch, d // width, nt),
        in_specs=[
            pl.BlockSpec(w3.shape, const2),
            pl.BlockSpec(masks.shape, lambda b, h, t: (0, 0, 0)),
            pl.BlockSpec((1, HEAD_DIM), const2),
            tile, tile, tile, tile, tile,
        ],
        out_specs=tile,
        out_shape=jax.ShapeDtypeStruct((m, d), BF16),
        scratch_shapes=[pltpu.VMEM((2, HEAD_DIM, HEAD_DIM), F32), pltpu.VMEM((c, width), F32)],
        compiler_params=pltpu.CompilerParams(
            dimension_semantics=("arbitrary", "arbitrary", "arbitrary"),
            vmem_limit_bytes=_vmem_limit(blk)),
        name="hgrn2_recurrence",
    )(w3, masks, gain.reshape(1, HEAD_DIM), q, k, v, log_f, gate)


def _bcast_mod(mod_l, idx, batch):
    d = mod_l.shape[-1] // N_MOD
    return mod_l[:batch, idx * d:(idx + 1) * d].reshape(batch, 1, d)


def _ffn(h, u, w_gu, w_down, gate, *, rows_per_batch):
    d, f2 = w_gu.shape
    f = f2 // 2
    tm, tn = 1024, 512
    act, = _mm(u, w_gu.astype(BF16), [0, f // tn], f, [], [BF16], _ep_swiglu,
               name="ffn_gate_up", tm=tm, tn=tn)
    return _residual_mm(act, w_down, h, gate, 0.5, rows_per_batch=rows_per_batch, name="ffn_down")


def _residual_mm(x, w, h, gate, coef, *, rows_per_batch, name):
    tm, tn = 1024, 1024
    tpb = rows_per_batch // tm
    extras = [
        (h, (tm, tn), lambda i, j, k: (i, j)),
        (gate, (1, 1, tn), lambda i, j, k: (i // tpb, 0, j)),
    ]
    out, = _mm(x, w.astype(BF16), [0], w.shape[1], extras, [F32],
               functools.partial(_ep_residual, coef=coef), name=name, tm=tm, tn=tn, tk=2048)
    return out


def kernel(x, c, ada_w, ada_b, norm_gains, ffn1_w_gu, ffn1_w_down, ffn2_w_gu, ffn2_w_down,
           attn_w_qkv, attn_w_o, attn_q_gain, attn_k_gain, attn_sinks,
           hgrn_w_qfig, hgrn_w_o, hgrn_g_gain, hgrn_lower_bounds):
    batch, seq, d = x.shape
    depth = ada_w.shape[0]
    m = batch * seq
    n_mixers = 2

    c_pad = jnp.zeros((8, d), F32).at[:batch].set(c)
    mod = _ada(c_pad, ada_w, ada_b)

    h = x.reshape(m, d)
    for i in range(depth):
        md = lambda idx: _bcast_mod(mod[i], idx, batch)
        u = _norm_mod(h, norm_gains[i, 0], md(0), md(1), rows_per_batch=seq)
        h = _ffn(h, u, ffn1_w_gu[i], ffn1_w_down[i], md(2), rows_per_batch=seq)
        u = _norm_mod(h, norm_gains[i, 1], md(3), md(4), rows_per_batch=seq)
        jm = i // n_mixers
        if i % n_mixers == 0:
            n_heads = attn_sinks.shape[1]
            w_qkv = attn_w_qkv[jm].astype(BF16)
            n_kv = (w_qkv.shape[1] // HEAD_DIM - n_heads) // 2
            tn = n_kv * HEAD_DIM
            q_tiles = n_heads // n_kv
            ones = jnp.ones((1, HEAD_DIM), F32)
            gains = jnp.stack([attn_q_gain[jm].reshape(1, HEAD_DIM) * HEAD_DIM ** -0.5] * q_tiles
                              + [attn_k_gain[jm].reshape(1, HEAD_DIM), ones])
            qkv, = _mm(u, w_qkv, [0], w_qkv.shape[1],
                       [(gains, (1, 1, HEAD_DIM), lambda i_, j_, k_: (j_, 0, 0))], [BF16],
                       functools.partial(_ep_qkv, n_norm_tiles=q_tiles + 1),
                       name="attn_qkv", tm=1024, tn=tn)
            heads = jnp.arange(1, n_heads + 1, dtype=F32)
            slopes = jnp.exp2(-8.0 * heads / n_heads)
            y = _attention(qkv, slopes, attn_sinks[jm].astype(F32), batch=batch, seq=seq,
                           n_heads=n_heads, n_kv=n_kv)
            h = _residual_mm(y, attn_w_o[jm], h, md(5), 1.0, rows_per_batch=seq, name="attn_out")
        else:
            w = hgrn_w_qfig[jm].astype(BF16)
            fd = (w.shape[1] - 2 * d) // 2
            tn = 1024
            qh, = _mm(u, w, [0], fd, [], [BF16], _ep_silu, name="hgrn_q", tn=tn)
            lbp = hgrn_lower_bounds.astype(F32)
            log_f, kh = _mm(u, w, [fd // tn], fd,
                            [(lbp, (depth, tn), lambda i_, j_, k_: (0, j_))], [F32, BF16],
                            functools.partial(_ep_forget, layer=i), name="hgrn_f", tn=tn)
            vh, = _mm(u, w, [2 * fd // tn], d, [], [BF16], _ep_cast, name="hgrn_i", tn=tn)
            gh, = _mm(u, w, [(2 * fd + d) // tn], d, [], [BF16], _ep_silu, name="hgrn_g", tn=tn)
            y = _hgrn_recurrence(qh, kh, vh, log_f, gh, hgrn_g_gain[jm], batch=batch, seq=seq)
            h = _residual_mm(y, hgrn_w_o[jm], h, md(5), 1.0, rows_per_batch=seq, name="hgrn_out")
        u = _norm_mod(h, norm_gains[i, 2], md(6), md(7), rows_per_batch=seq)
        h = _ffn(h, u, ffn2_w_gu[i], ffn2_w_down[i], md(8), rows_per_batch=seq)
    return h.reshape(batch, seq, d)
```

```python
import functools

import numpy as np
import jax
import jax.numpy as jnp
from jax import lax
from jax.experimental import pallas as pl
from jax.experimental.pallas import tpu as pltpu

F32 = jnp.float32
BF16 = jnp.bfloat16
EPS = 1e-6
LOG2_E = 1.4426950408889634

WINDOW = 128
HEAD_DIM = 128
N_MOD = 9
HGRN_CHUNK = 128

V7X_VMEM_BYTES = 64 * 1024 * 1024
VMEM_HEADROOM_BYTES = 6 * 1024 * 1024
LANES = 128


def _vmem_limit(block_bytes):
    want = 2 * block_bytes + VMEM_HEADROOM_BYTES
    return int(min(max(want, 32 * 1024 * 1024), V7X_VMEM_BYTES - 4 * 1024 * 1024))


def _nbytes(shape, dtype):
    return int(np.prod(shape)) * jnp.dtype(dtype).itemsize


def _silu(x):
    return x * jax.nn.sigmoid(x)


def _dot(a, b):
    return jnp.dot(a, b, preferred_element_type=F32)


def _dot_nt(a, b):
    return lax.dot_general(a, b, (((1,), (1,)), ((), ())), preferred_element_type=F32)


def _dot_tn(a, b):
    return lax.dot_general(a, b, (((0,), (0,)), ((), ())), preferred_element_type=F32)


def _ada_kernel(c_ref, w_ref, b_ref, o_ref):
    cond = _silu(c_ref[...]).astype(BF16)
    o_ref[0] = _dot(cond, w_ref[0].astype(BF16)) + b_ref[0]


def _ada(c_pad, ada_w, ada_b, *, tn=512):
    depth, d, nd = ada_w.shape
    rows = c_pad.shape[0]
    blk = _nbytes((d, tn), F32) + _nbytes((rows, tn), F32) + _nbytes((rows, d), F32)
    return pl.pallas_call(
        _ada_kernel,
        grid=(depth, nd // tn),
        in_specs=[
            pl.BlockSpec((rows, d), lambda l, j: (0, 0)),
            pl.BlockSpec((1, d, tn), lambda l, j: (l, 0, j)),
            pl.BlockSpec((1, 1, tn), lambda l, j: (l, 0, j)),
        ],
        out_specs=pl.BlockSpec((1, rows, tn), lambda l, j: (l, 0, j)),
        out_shape=jax.ShapeDtypeStruct((depth, rows, nd), F32),
        compiler_params=pltpu.CompilerParams(
            dimension_semantics=("arbitrary", "arbitrary"),
            vmem_limit_bytes=_vmem_limit(blk)),
        name="ada_mod",
    )(c_pad, ada_w, ada_b.reshape(depth, 1, nd))


def _norm_mod_kernel(h_ref, gain_ref, shift_ref, scale_ref, o_ref):
    h = h_ref[...]
    y = h * lax.rsqrt(jnp.mean(h * h, axis=-1, keepdims=True) + EPS) * gain_ref[...]
    o_ref[...] = (y * (1.0 + scale_ref[0]) + shift_ref[0]).astype(o_ref.dtype)


def _norm_mod(h, gain, shift, scale, *, rows_per_batch, tm=512):
    m, d = h.shape
    tpb = rows_per_batch // tm
    blk = _nbytes((tm, d), F32) + _nbytes((tm, d), BF16)
    return pl.pallas_call(
        _norm_mod_kernel,
        grid=(m // tm,),
        in_specs=[
            pl.BlockSpec((tm, d), lambda i: (i, 0)),
            pl.BlockSpec((1, d), lambda i: (0, 0)),
            pl.BlockSpec((1, 1, d), lambda i: (i // tpb, 0, 0)),
            pl.BlockSpec((1, 1, d), lambda i: (i // tpb, 0, 0)),
        ],
        out_specs=pl.BlockSpec((tm, d), lambda i: (i, 0)),
        out_shape=jax.ShapeDtypeStruct((m, d), BF16),
        compiler_params=pltpu.CompilerParams(
            dimension_semantics=("arbitrary",),
            vmem_limit_bytes=_vmem_limit(blk)),
        name="norm_mod",
    )(h, gain.reshape(1, d), shift, scale)


def _mm_kernel(*refs, n_w, n_e, n_o, nk, has_cast, epilogue):
    x_ref = refs[0]
    w_refs = refs[1:1 + n_w]
    e_refs = refs[1 + n_w:1 + n_w + n_e]
    n_in = 1 + n_w + n_e + has_cast
    o_refs = refs[n_in:n_in + n_o]
    acc_refs = refs[n_in + n_o + has_cast:]
    j = pl.program_id(1)

    if has_cast:
        cast_out = refs[n_in + n_o]
        cast_out[...] = refs[n_in - 1][...].astype(cast_out.dtype)

    def dots():
        x = x_ref[...]
        return [_dot(x, w[...]) for w in w_refs]

    if nk == 1:
        epilogue(dots(), j, e_refs, o_refs)
        return

    k = pl.program_id(2)

    @pl.when(k == 0)
    def _():
        for a, d in zip(acc_refs, dots()):
            a[...] = d

    if nk > 2:
        @pl.when((k > 0) & (k < nk - 1))
        def _():
            for a, d in zip(acc_refs, dots()):
                a[...] += d

    @pl.when(k == nk - 1)
    def _():
        epilogue([a[...] + d for a, d in zip(acc_refs, dots())], j, e_refs, o_refs)


def _cast_chunk_rows(n_rows, n_steps):
    rows = 16
    while n_rows % rows or n_rows // rows > n_steps:
        rows += 16
    return rows


def _mm(x, w, w_col_blocks, n_cols, extras, outs, epilogue, *, name, cast=None,
        tm=1024, tn=1024, tk=4096):
    m, kdim = x.shape
    tk = min(tk, kdim)
    nk = kdim // tk
    grid = (m // tm, n_cols // tn, nk)
    in_specs = [pl.BlockSpec((tm, tk), lambda i, j, k: (i, k))]
    blk = _nbytes((tm, tk), x.dtype)
    for off in w_col_blocks:
        in_specs.append(pl.BlockSpec((tk, tn), lambda i, j, k, off=off: (k, j + off)))
        blk += _nbytes((tk, tn), w.dtype)
    for arr, bshape, imap in extras:
        in_specs.append(pl.BlockSpec(bshape, imap))
        blk += _nbytes(bshape, arr.dtype)
    out_specs = [pl.BlockSpec((tm, tn), lambda i, j, k: (i, j)) for _ in outs]
    out_shape = [jax.ShapeDtypeStruct((m, n_cols), dt) for dt in outs]
    blk += sum(_nbytes((tm, tn), dt) for dt in outs)
    operands = [x, *([w] * len(w_col_blocks)), *[e[0] for e in extras]]
    if cast is not None:
        src, layer = cast
        _, n_rows, n_src_cols = src.shape
        rows = _cast_chunk_rows(n_rows, grid[0] * grid[1] * grid[2])
        last = n_rows // rows - 1
        chunk = lambda i, j, k: jnp.minimum((i * grid[1] + j) * grid[2] + k, last)
        in_specs.append(pl.BlockSpec((None, rows, n_src_cols), lambda i, j, k: (layer, chunk(i, j, k), 0)))
        out_specs.append(pl.BlockSpec((rows, n_src_cols), lambda i, j, k: (chunk(i, j, k), 0)))
        out_shape.append(jax.ShapeDtypeStruct((n_rows, n_src_cols), BF16))
        operands.append(src)
        blk += _nbytes((rows, n_src_cols), F32) + _nbytes((rows, n_src_cols), BF16)
    n_w = len(w_col_blocks)
    scratch = [pltpu.VMEM((tm, tn), F32) for _ in range(n_w)] if nk > 1 else []
    scratch_bytes = (len(scratch) + 3 * n_w) * _nbytes((tm, tn), F32)
    kern = functools.partial(_mm_kernel, n_w=n_w, n_e=len(extras), n_o=len(outs), nk=nk,
                             has_cast=int(cast is not None), epilogue=epilogue)
    return pl.pallas_call(
        kern,
        grid=grid,
        in_specs=in_specs,
        out_specs=out_specs,
        out_shape=out_shape,
        scratch_shapes=scratch,
        compiler_params=pltpu.CompilerParams(
            dimension_semantics=("arbitrary", "arbitrary", "arbitrary"),
            vmem_limit_bytes=_vmem_limit(blk + scratch_bytes // 2)),
        name=name,
    )(*operands)


def _cast_kernel(src_ref, o_ref):
    o_ref[...] = src_ref[...].astype(o_ref.dtype)


def _cast_bf16(src, layer, *, rows=128):
    _, n_rows, n_cols = src.shape
    blk = _nbytes((rows, n_cols), F32) + _nbytes((rows, n_cols), BF16)
    return pl.pallas_call(
        _cast_kernel,
        grid=(n_rows // rows,),
        in_specs=[pl.BlockSpec((None, rows, n_cols), lambda r: (layer, r, 0))],
        out_specs=pl.BlockSpec((rows, n_cols), lambda r: (r, 0)),
        out_shape=jax.ShapeDtypeStruct((n_rows, n_cols), BF16),
        compiler_params=pltpu.CompilerParams(
            dimension_semantics=("arbitrary",),
            vmem_limit_bytes=_vmem_limit(blk)),
        name="cast_weights",
    )(src)


def _ep_swiglu(accs, j, e_refs, o_refs):
    a, b = accs
    o_refs[0][...] = (_silu(a) * b).astype(o_refs[0].dtype)


def _ep_residual(accs, j, e_refs, o_refs, *, coef):
    h_ref, gate_ref = e_refs
    o_refs[0][...] = h_ref[...] + (coef * gate_ref[0]) * accs[0]


def _ep_qkv(accs, j, e_refs, o_refs, *, n_norm_tiles):
    y = accs[0]
    gain_ref, = e_refs
    o_ref = o_refs[0]
    tn = y.shape[1]

    @pl.when(j < n_norm_tiles)
    def _():
        g = gain_ref[0]
        for hd in range(tn // HEAD_DIM):
            yh = y[:, hd * HEAD_DIM:(hd + 1) * HEAD_DIM]
            inv = lax.rsqrt(jnp.mean(yh * yh, axis=-1, keepdims=True) + EPS)
            o_ref[:, hd * HEAD_DIM:(hd + 1) * HEAD_DIM] = (yh * inv * g).astype(o_ref.dtype)

    @pl.when(j >= n_norm_tiles)
    def _():
        o_ref[...] = y.astype(o_ref.dtype)


def _ep_silu(accs, j, e_refs, o_refs):
    o_refs[0][...] = _silu(accs[0]).astype(o_refs[0].dtype)


def _ep_cast(accs, j, e_refs, o_refs):
    o_refs[0][...] = accs[0].astype(o_refs[0].dtype)


def _ep_forget(accs, j, e_refs, o_refs, *, layer):
    f = accs[0]
    lbp = e_refs[0][...]
    depth = lbp.shape[0]
    mx = lbp[0:1]
    for l in range(1, depth):
        mx = jnp.maximum(mx, lbp[l:l + 1])
    es = [jnp.exp(lbp[l:l + 1] - mx) for l in range(depth)]
    tot = es[0]
    for l in range(1, depth):
        tot = tot + es[l]
    s = [e / tot for e in es]
    cum = s[0]
    for l in range(1, layer + 1):
        cum = cum + s[l]
    lb = cum - s[0]
    log_sig = jnp.minimum(f, 0.0) - jnp.log1p(jnp.exp(-jnp.abs(f)))
    a = jnp.log(lb)
    b = jnp.log1p(-lb) + log_sig
    hi = jnp.maximum(a, b)
    lo = jnp.minimum(a, b)
    log_f = hi + jnp.log1p(jnp.exp(lo - hi))
    o_refs[0][...] = log_f * LOG2_E
    o_refs[1][...] = ((1.0 - lb) * jax.nn.sigmoid(-f)).astype(o_refs[1].dtype)


def _split3(x):
    hi = x.astype(BF16)
    r = x - hi.astype(F32)
    mid = r.astype(BF16)
    lo = (r - mid.astype(F32)).astype(BF16)
    return hi, mid, lo


def _attn_tables(sinks, *, n_kv, group, blk):
    n_heads = n_kv * group
    q_feat = jnp.zeros((n_heads, blk, LANES), BF16)
    for i, s_i in enumerate(_split3(sinks)):
        q_feat = q_feat.at[:, :, i].set(jnp.broadcast_to(s_i[:, None], (n_heads, blk)))
    q_feat = q_feat.reshape(n_kv, group * blk, LANES)
    k_feat = np.zeros((2 * blk, LANES), np.float32)
    k_feat[0, 0:3] = 1.0
    qi = np.arange(blk)[:, None]
    kj = np.arange(2 * blk)[None, :]
    dist = (qi + blk - kj).astype(np.float32)
    valid = (dist >= 0) & (dist < WINDOW)
    sink_col = kj == 0
    inf = np.float32(np.inf)
    dist_all = np.where(sink_col, np.float32(0), np.where(valid, dist, inf))
    dist_first = np.where(sink_col, np.float32(0), np.where(valid & (kj >= blk), dist, inf))
    return q_feat, jnp.asarray(k_feat, BF16), jnp.asarray(dist_all), jnp.asarray(dist_first)


def _attn_kernel(slope_ref, qf_ref, kf_ref, dist_ref, dist_first_ref, q_ref, kc_ref, kp_ref, vc_ref,
                 vp_ref, o_ref, d_scr, *, n_kv, group):
    blk = q_ref.shape[0]
    d_scr[...] = jnp.where(pl.program_id(1) > 0, dist_ref[...], dist_first_ref[...])
    keep = (lax.broadcasted_iota(jnp.int32, (16, HEAD_DIM), 0) != 0).astype(BF16)
    ones = jnp.ones((2 * blk, HEAD_DIM), BF16)
    k_feat = kf_ref[...]

    def kv_head(hk, carry):
        ko = pl.ds(pl.multiple_of(hk * HEAD_DIM, HEAD_DIM), HEAD_DIM)
        kp = kp_ref[:, ko]
        vp = vp_ref[:, ko]
        vp = jnp.concatenate([vp[:16] * keep, vp[16:]], axis=0)
        kp = jnp.concatenate([kp[:16] * keep, kp[16:]], axis=0)
        k2 = jnp.concatenate([jnp.concatenate([kp, kc_ref[:, ko]], axis=0), k_feat], axis=1)
        v2 = jnp.concatenate([jnp.concatenate([vp, vc_ref[:, ko]], axis=0), ones], axis=1)
        cols = [pl.ds(pl.multiple_of((hk * group + g) * HEAD_DIM, HEAD_DIM), HEAD_DIM)
                for g in range(group)]
        q4 = jnp.concatenate([q_ref[:, cg] for cg in cols], axis=0)
        q4 = jnp.concatenate([q4, qf_ref[hk]], axis=1)
        s = _dot_nt(q4, k2)
        dist = d_scr[...]
        logits = jnp.concatenate(
            [s[g * blk:(g + 1) * blk] - slope_ref[hk * group + g] * dist for g in range(group)],
            axis=0)
        m = jnp.max(logits, axis=-1, keepdims=True)
        p = jnp.exp(logits - m).astype(BF16)
        o = _dot(p, v2)
        out = o[:, :HEAD_DIM] / o[:, HEAD_DIM:]
        for g, cg in enumerate(cols):
            o_ref[:, cg] = out[g * blk:(g + 1) * blk].astype(o_ref.dtype)
        return carry

    lax.fori_loop(0, n_kv, kv_head, 0, unroll=2)


def _attention(qkv, slopes, sinks, *, batch, seq, n_heads, n_kv):
    m = qkv.shape[0]
    blk = WINDOW
    group = n_heads // n_kv
    nb = seq // blk
    dq = n_heads * HEAD_DIM
    dkv = n_kv * HEAD_DIM
    k_col = dq // dkv
    v_col = k_col + 1
    q_feat, k_feat, dist, dist_first = _attn_tables(sinks, n_kv=n_kv, group=group, blk=blk)
    cur = lambda b, n: b * nb + n
    prev = lambda b, n: b * nb + jnp.maximum(n - 1, 0)
    const2 = lambda b, n: (0, 0)
    blk_bytes = (2 * _nbytes((blk, dq), BF16) + 4 * _nbytes((blk, dkv), BF16)
                 + _nbytes(q_feat.shape, BF16) + 3 * _nbytes(dist.shape, F32))
    return pl.pallas_call(
        functools.partial(_attn_kernel, n_kv=n_kv, group=group),
        grid=(batch, nb),
        in_specs=[
            pl.BlockSpec(memory_space=pltpu.SMEM),
            pl.BlockSpec(q_feat.shape, lambda b, n: (0, 0, 0)),
            pl.BlockSpec(k_feat.shape, const2),
            pl.BlockSpec(dist.shape, const2),
            pl.BlockSpec(dist.shape, const2),
            pl.BlockSpec((blk, dq), lambda b, n: (cur(b, n), 0)),
            pl.BlockSpec((blk, dkv), lambda b, n: (cur(b, n), k_col)),
            pl.BlockSpec((blk, dkv), lambda b, n: (prev(b, n), k_col)),
            pl.BlockSpec((blk, dkv), lambda b, n: (cur(b, n), v_col)),
            pl.BlockSpec((blk, dkv), lambda b, n: (prev(b, n), v_col)),
        ],
        out_specs=pl.BlockSpec((blk, dq), lambda b, n: (cur(b, n), 0)),
        out_shape=jax.ShapeDtypeStruct((m, dq), BF16),
        scratch_shapes=[pltpu.VMEM(dist.shape, F32)],
        compiler_params=pltpu.CompilerParams(
            dimension_semantics=("arbitrary", "arbitrary"),
            vmem_limit_bytes=_vmem_limit(blk_bytes)),
        name="swa_attention",
    )(slopes, q_feat, k_feat, dist, dist_first, qkv, qkv, qkv, qkv, qkv)


def _hgrn_tables(c):
    halves = [c >> (lv + 1) for lv in range(int(np.log2(c)))]
    r = np.arange(c)[:, None]
    j = np.arange(c)[None, :]

    def partial_sums(half):
        ref = (r // (2 * half)) * (2 * half) + half - 1
        return np.where(r > ref, (j > ref) & (j <= r), (j > r) & (j <= ref))

    masks = [((r // (2 * h)) == (j // (2 * h))) & ((r % (2 * h)) >= h) & ((j % (2 * h)) < h)
             for h in halves]
    masks.append(r == j)
    w = np.concatenate([j <= r, partial_sums(4), partial_sums(2)], axis=0).astype(np.float32)
    w3 = np.concatenate([w, w, w], axis=1)
    return jnp.asarray(w3, BF16), jnp.asarray(np.stack(masks).astype(np.float32), BF16), halves


def _hgrn_kernel(w3_ref, mask_ref, gain_ref, q_ref, k_ref, v_ref, lf_ref, gate_ref, o_ref,
                 s_ref, d_scr, *, c, halves):
    @pl.when(pl.program_id(2) == 0)
    def _():
        s_ref[...] = jnp.zeros_like(s_ref)

    n_chunks = q_ref.shape[0] // c
    zero_h = jnp.zeros((c, LANES), BF16)
    n_lv = len(halves)

    def block_diag(x0, x1):
        return jnp.concatenate([jnp.concatenate([x0, zero_h], axis=1),
                                jnp.concatenate([zero_h, x1], axis=1)], axis=0)

    def both(x):
        return x[:, :LANES], x[:, LANES:]

    def pair_mask(i):
        mk = mask_ref[i]
        return jnp.concatenate([mk, mk], axis=1)

    def chunk(ci, carry):
        rows = pl.ds(pl.multiple_of(ci * c, c), c)
        q = lambda: q_ref[rows, :]
        k = lambda: k_ref[rows, :]
        lf3 = jnp.concatenate(_split3(lf_ref[rows, :]), axis=0)
        d_scr[...] = _dot(w3_ref[...], lf3)
        b = lambda: d_scr[0:c, :]

        def level_exponent(half):
            pieces = []
            for g0 in range(0, c, 2 * half):
                ref = jnp.broadcast_to(d_scr[g0 + half - 1:g0 + half, :], (half, 2 * LANES))
                pieces += [ref - d_scr[g0:g0 + half, :], d_scr[g0 + half:g0 + 2 * half, :] - ref]
            return jnp.concatenate(pieces, axis=0)

        e1 = jnp.exp2(lf_ref[rows, :]).astype(BF16)
        p = _dot_nt(jnp.concatenate([q(), q() * e1], axis=0), block_diag(*both(k())))
        a = p[0:c].astype(BF16) * pair_mask(n_lv) + p[c:2 * c].astype(BF16) * pair_mask(n_lv - 1)
        for li, half in enumerate(halves[:-1]):
            if half >= 8:
                dn = level_exponent(half)
            else:
                dn = d_scr[c:2 * c, :] if half == 4 else d_scr[2 * c:3 * c, :]
            e = jnp.exp2(dn).astype(BF16)
            a = a + _dot_nt(q() * e, block_diag(*both(k() * e))).astype(BF16) * pair_mask(li)
        intra = _dot(a, block_diag(*both(v_ref[rows, :])))

        s0 = s_ref[0]
        s1 = s_ref[1]
        inter = _dot_nt(q() * jnp.exp2(b()).astype(BF16),
                        block_diag(s0.astype(BF16), s1.astype(BF16)))
        o = inter + intra

        b_last = d_scr[c - 1:c, :]
        u = _dot_tn(v_ref[rows, :], k() * jnp.exp2(b_last - b()).astype(BF16))
        decay = jnp.exp2(b_last)
        s_ref[0] = s0 * decay[:, :LANES] + u[:LANES, :LANES]
        s_ref[1] = s1 * decay[:, LANES:] + u[LANES:, LANES:]

        gate = gate_ref[rows, :].astype(F32)
        gain = gain_ref[...]
        for hh in range(2):
            lanes = slice(hh * LANES, (hh + 1) * LANES)
            oh = o[:, lanes]
            y = oh * lax.rsqrt(jnp.mean(oh * oh, axis=-1, keepdims=True) + EPS) * gain
            o_ref[rows, lanes] = (y * gate[:, lanes]).astype(o_ref.dtype)
        return carry

    lax.fori_loop(0, n_chunks, chunk, 0, unroll=2)


def _hgrn_recurrence(q, k, v, log_f, gate, gain, *, batch, seq, tb=1024):
    m, d = q.shape
    c = HGRN_CHUNK
    w3, masks, halves = _hgrn_tables(c)
    nt = seq // tb
    width = 2 * HEAD_DIM
    tile = pl.BlockSpec((tb, width), lambda b, h, t: (b * nt + t, h))
    const2 = lambda b, h, t: (0, 0)
    blk = (_nbytes((tb, width), F32) + 5 * _nbytes((tb, width), BF16)
           + _nbytes(w3.shape, BF16) + _nbytes(masks.shape, BF16))
    return pl.pallas_call(
        functools.partial(_hgrn_kernel, c=c, halves=halves),
        grid=(batch, d // width, nt),
        in_specs=[
            pl.BlockSpec(w3.shape, const2),
            pl.BlockSpec(masks.shape, lambda b, h, t: (0, 0, 0)),
            pl.BlockSpec((1, HEAD_DIM), const2),
            tile, tile, tile, tile, tile,
        ],
        out_specs=tile,
        out_shape=jax.ShapeDtypeStruct((m, d), BF16),
        scratch_shapes=[pltpu.VMEM((2, HEAD_DIM, HEAD_DIM), F32), pltpu.VMEM((3 * c, width), F32)],
        compiler_params=pltpu.CompilerParams(
            dimension_semantics=("arbitrary", "arbitrary", "arbitrary"),
            vmem_limit_bytes=_vmem_limit(blk)),
        name="hgrn2_recurrence",
    )(w3, masks, gain.reshape(1, HEAD_DIM), q, k, v, log_f, gate)


def _bcast_mod(mod_l, idx, batch):
    d = mod_l.shape[-1] // N_MOD
    return mod_l[:batch, idx * d:(idx + 1) * d].reshape(batch, 1, d)


def _gate_up(u, w_gu, cast):
    f = w_gu.shape[1] // 2
    tn = 512
    return _mm(u, w_gu, [0, f // tn], f, [], [BF16], _ep_swiglu, name="ffn_gate_up",
               cast=cast, tm=1024, tn=tn)


def _residual_mm(x, w, h, gate, coef, cast, *, rows_per_batch, name):
    tm, tn = 1024, 1024
    tpb = rows_per_batch // tm
    extras = [
        (h, (tm, tn), lambda i, j, k: (i, j)),
        (gate, (1, 1, tn), lambda i, j, k: (i // tpb, 0, j)),
    ]
    return _mm(x, w, [0], w.shape[1], extras, [F32], functools.partial(_ep_residual, coef=coef),
               name=name, cast=cast, tm=tm, tn=tn, tk=2048)


def kernel(x, c, ada_w, ada_b, norm_gains, ffn1_w_gu, ffn1_w_down, ffn2_w_gu, ffn2_w_down,
           attn_w_qkv, attn_w_o, attn_q_gain, attn_k_gain, attn_sinks,
           hgrn_w_qfig, hgrn_w_o, hgrn_g_gain, hgrn_lower_bounds):
    batch, seq, d = x.shape
    depth = ada_w.shape[0]
    m = batch * seq
    n_mixers = 2

    c_pad = jnp.zeros((8, d), F32).at[:batch].set(c)
    mod = _ada(c_pad, ada_w, ada_b)

    w_gu = _cast_bf16(ffn1_w_gu, 0)
    h = x.reshape(m, d)
    for i in range(depth):
        md = lambda idx: _bcast_mod(mod[i], idx, batch)
        jm = i // n_mixers
        is_attn = i % n_mixers == 0
        u = _norm_mod(h, norm_gains[i, 0], md(0), md(1), rows_per_batch=seq)
        act, w_down = _gate_up(u, w_gu, (ffn1_w_down, i))
        h, w_mix = _residual_mm(act, w_down, h, md(2), 0.5,
                                (attn_w_qkv, jm) if is_attn else (hgrn_w_qfig, jm),
                                rows_per_batch=seq, name="ffn_down")
        u = _norm_mod(h, norm_gains[i, 1], md(3), md(4), rows_per_batch=seq)
        if is_attn:
            n_heads = attn_sinks.shape[1]
            n_qkv = w_mix.shape[1]
            n_kv = (n_qkv // HEAD_DIM - n_heads) // 2
            tn = n_kv * HEAD_DIM
            q_tiles = n_heads // n_kv
            ones = jnp.ones((1, HEAD_DIM), F32)
            gains = jnp.stack([attn_q_gain[jm].reshape(1, HEAD_DIM) * HEAD_DIM ** -0.5] * q_tiles
                              + [attn_k_gain[jm].reshape(1, HEAD_DIM), ones])
            qkv, w_o = _mm(u, w_mix, [0], n_qkv,
                           [(gains, (1, 1, HEAD_DIM), lambda i_, j_, k_: (j_, 0, 0))], [BF16],
                           functools.partial(_ep_qkv, n_norm_tiles=q_tiles + 1),
                           name="attn_qkv", cast=(attn_w_o, jm), tm=1024, tn=tn)
            heads = jnp.arange(1, n_heads + 1, dtype=F32)
            slopes = jnp.exp2(-8.0 * heads / n_heads)
            y = _attention(qkv, slopes, attn_sinks[jm].astype(F32), batch=batch, seq=seq,
                           n_heads=n_heads, n_kv=n_kv)
            mixer_name = "attn_out"
        else:
            fd = (w_mix.shape[1] - 2 * d) // 2
            tn = 1024
            qh, w_o = _mm(u, w_mix, [0], fd, [], [BF16], _ep_silu, name="hgrn_q",
                          cast=(hgrn_w_o, jm), tn=tn)
            lbp = hgrn_lower_bounds.astype(F32)
            log_f, kh = _mm(u, w_mix, [fd // tn], fd,
                            [(lbp, (depth, tn), lambda i_, j_, k_: (0, j_))], [F32, BF16],
                            functools.partial(_ep_forget, layer=i), name="hgrn_f", tn=tn)
            vh, = _mm(u, w_mix, [2 * fd // tn], d, [], [BF16], _ep_cast, name="hgrn_i", tn=tn)
            gh, = _mm(u, w_mix, [(2 * fd + d) // tn], d, [], [BF16], _ep_silu, name="hgrn_g", tn=tn)
            y = _hgrn_recurrence(qh, kh, vh, log_f, gh, hgrn_g_gain[jm], batch=batch, seq=seq)
            mixer_name = "hgrn_out"
        h, w_gu = _residual_mm(y, w_o, h, md(5), 1.0, (ffn2_w_gu, i), rows_per_batch=seq,
                               name=mixer_name)
        u = _norm_mod(h, norm_gains[i, 2], md(6), md(7), rows_per_batch=seq)
        act, w_down = _gate_up(u, w_gu, (ffn2_w_down, i))
        if i + 1 < depth:
            h, w_gu = _residual_mm(act, w_down, h, md(8), 0.5, (ffn1_w_gu, i + 1),
                                   rows_per_batch=seq, name="ffn_down")
        else:
            h, = _residual_mm(act, w_down, h, md(8), 0.5, None, rows_per_batch=seq, name="ffn_down")
    return h.reshape(batch, seq, d)
```

```python
import functools

import numpy as np
import jax
import jax.numpy as jnp
from jax import lax
from jax.experimental import pallas as pl
from jax.experimental.pallas import tpu as pltpu

F32 = jnp.float32
BF16 = jnp.bfloat16
EPS = 1e-6
LOG2_E = 1.4426950408889634

WINDOW = 128
HEAD_DIM = 128
N_MOD = 9
HGRN_CHUNK = 128

V7X_VMEM_BYTES = 64 * 1024 * 1024
VMEM_HEADROOM_BYTES = 6 * 1024 * 1024
LANES = 128


def _vmem_limit(block_bytes):
    want = 2 * block_bytes + VMEM_HEADROOM_BYTES
    return int(min(max(want, 32 * 1024 * 1024), V7X_VMEM_BYTES - 4 * 1024 * 1024))


def _nbytes(shape, dtype):
    return int(np.prod(shape)) * jnp.dtype(dtype).itemsize


def _silu(x):
    return x * jax.nn.sigmoid(x)


def _dot(a, b):
    return jnp.dot(a, b, preferred_element_type=F32)


def _dot_nt(a, b):
    return lax.dot_general(a, b, (((1,), (1,)), ((), ())), preferred_element_type=F32)


def _dot_tn(a, b):
    return lax.dot_general(a, b, (((0,), (0,)), ((), ())), preferred_element_type=F32)


def _ada_kernel(c_ref, w_ref, b_ref, o_ref):
    cond = _silu(c_ref[...]).astype(BF16)
    o_ref[0] = _dot(cond, w_ref[0].astype(BF16)) + b_ref[0]


def _ada(c_pad, ada_w, ada_b, *, tn=512):
    depth, d, nd = ada_w.shape
    rows = c_pad.shape[0]
    blk = _nbytes((d, tn), F32) + _nbytes((rows, tn), F32) + _nbytes((rows, d), F32)
    return pl.pallas_call(
        _ada_kernel,
        grid=(depth, nd // tn),
        in_specs=[
            pl.BlockSpec((rows, d), lambda l, j: (0, 0)),
            pl.BlockSpec((1, d, tn), lambda l, j: (l, 0, j)),
            pl.BlockSpec((1, 1, tn), lambda l, j: (l, 0, j)),
        ],
        out_specs=pl.BlockSpec((1, rows, tn), lambda l, j: (l, 0, j)),
        out_shape=jax.ShapeDtypeStruct((depth, rows, nd), F32),
        compiler_params=pltpu.CompilerParams(
            dimension_semantics=("arbitrary", "arbitrary"),
            vmem_limit_bytes=_vmem_limit(blk)),
        name="ada_mod",
    )(c_pad, ada_w, ada_b.reshape(depth, 1, nd))


def _norm_mod_kernel(h_ref, gain_ref, shift_ref, scale_ref, o_ref):
    h = h_ref[...]
    y = h * lax.rsqrt(jnp.mean(h * h, axis=-1, keepdims=True) + EPS) * gain_ref[...]
    o_ref[...] = (y * (1.0 + scale_ref[0]) + shift_ref[0]).astype(o_ref.dtype)


def _norm_mod(h, gain, shift, scale, *, rows_per_batch, tm=512):
    m, d = h.shape
    tpb = rows_per_batch // tm
    blk = _nbytes((tm, d), F32) + _nbytes((tm, d), BF16)
    return pl.pallas_call(
        _norm_mod_kernel,
        grid=(m // tm,),
        in_specs=[
            pl.BlockSpec((tm, d), lambda i: (i, 0)),
            pl.BlockSpec((1, d), lambda i: (0, 0)),
            pl.BlockSpec((1, 1, d), lambda i: (i // tpb, 0, 0)),
            pl.BlockSpec((1, 1, d), lambda i: (i // tpb, 0, 0)),
        ],
        out_specs=pl.BlockSpec((tm, d), lambda i: (i, 0)),
        out_shape=jax.ShapeDtypeStruct((m, d), BF16),
        compiler_params=pltpu.CompilerParams(
            dimension_semantics=("arbitrary",),
            vmem_limit_bytes=_vmem_limit(blk)),
        name="norm_mod",
    )(h, gain.reshape(1, d), shift, scale)


def _mm_kernel(*refs, n_w, n_e, n_o, nk, has_cast, epilogue):
    x_ref = refs[0]
    w_refs = refs[1:1 + n_w]
    e_refs = refs[1 + n_w:1 + n_w + n_e]
    n_in = 1 + n_w + n_e + has_cast
    o_refs = refs[n_in:n_in + n_o]
    acc_refs = refs[n_in + n_o + has_cast:]
    j = pl.program_id(1)

    if has_cast:
        cast_out = refs[n_in + n_o]
        cast_out[...] = refs[n_in - 1][...].astype(cast_out.dtype)

    def dots():
        x = x_ref[...]
        return [_dot(x, w[...]) for w in w_refs]

    if nk == 1:
        epilogue(dots(), j, e_refs, o_refs)
        return

    k = pl.program_id(2)

    @pl.when(k == 0)
    def _():
        for a, d in zip(acc_refs, dots()):
            a[...] = d

    if nk > 2:
        @pl.when((k > 0) & (k < nk - 1))
        def _():
            for a, d in zip(acc_refs, dots()):
                a[...] += d

    @pl.when(k == nk - 1)
    def _():
        epilogue([a[...] + d for a, d in zip(acc_refs, dots())], j, e_refs, o_refs)


def _cast_chunk_rows(n_rows, n_steps):
    rows = 16
    while n_rows % rows or n_rows // rows > n_steps:
        rows += 16
    return rows


def _mm(x, w, w_col_blocks, n_cols, extras, outs, epilogue, *, name, cast=None,
        tm=1024, tn=1024, tk=4096):
    m, kdim = x.shape
    tk = min(tk, kdim)
    nk = kdim // tk
    grid = (m // tm, n_cols // tn, nk)
    in_specs = [pl.BlockSpec((tm, tk), lambda i, j, k: (i, k))]
    blk = _nbytes((tm, tk), x.dtype)
    for off in w_col_blocks:
        in_specs.append(pl.BlockSpec((tk, tn), lambda i, j, k, off=off: (k, j + off)))
        blk += _nbytes((tk, tn), w.dtype)
    for arr, bshape, imap in extras:
        in_specs.append(pl.BlockSpec(bshape, imap))
        blk += _nbytes(bshape, arr.dtype)
    out_specs = [pl.BlockSpec((tm, tn), lambda i, j, k: (i, j)) for _ in outs]
    out_shape = [jax.ShapeDtypeStruct((m, n_cols), dt) for dt in outs]
    blk += sum(_nbytes((tm, tn), dt) for dt in outs)
    operands = [x, *([w] * len(w_col_blocks)), *[e[0] for e in extras]]
    if cast is not None:
        src, layer = cast
        _, n_rows, n_src_cols = src.shape
        rows = _cast_chunk_rows(n_rows, grid[0] * grid[1] * grid[2])
        last = n_rows // rows - 1
        chunk = lambda i, j, k: jnp.minimum((i * grid[1] + j) * grid[2] + k, last)
        in_specs.append(pl.BlockSpec((None, rows, n_src_cols), lambda i, j, k: (layer, chunk(i, j, k), 0)))
        out_specs.append(pl.BlockSpec((rows, n_src_cols), lambda i, j, k: (chunk(i, j, k), 0)))
        out_shape.append(jax.ShapeDtypeStruct((n_rows, n_src_cols), BF16))
        operands.append(src)
        blk += _nbytes((rows, n_src_cols), F32) + _nbytes((rows, n_src_cols), BF16)
    n_w = len(w_col_blocks)
    scratch = [pltpu.VMEM((tm, tn), F32) for _ in range(n_w)] if nk > 1 else []
    scratch_bytes = (len(scratch) + 3 * n_w) * _nbytes((tm, tn), F32)
    kern = functools.partial(_mm_kernel, n_w=n_w, n_e=len(extras), n_o=len(outs), nk=nk,
                             has_cast=int(cast is not None), epilogue=epilogue)
    return pl.pallas_call(
        kern,
        grid=grid,
        in_specs=in_specs,
        out_specs=out_specs,
        out_shape=out_shape,
        scratch_shapes=scratch,
        compiler_params=pltpu.CompilerParams(
            dimension_semantics=("arbitrary", "arbitrary", "arbitrary"),
            vmem_limit_bytes=_vmem_limit(blk + scratch_bytes // 2)),
        name=name,
    )(*operands)


def _cast_kernel(src_ref, o_ref):
    o_ref[...] = src_ref[...].astype(o_ref.dtype)


def _cast_bf16(src, layer, *, rows=128):
    _, n_rows, n_cols = src.shape
    blk = _nbytes((rows, n_cols), F32) + _nbytes((rows, n_cols), BF16)
    return pl.pallas_call(
        _cast_kernel,
        grid=(n_rows // rows,),
        in_specs=[pl.BlockSpec((None, rows, n_cols), lambda r: (layer, r, 0))],
        out_specs=pl.BlockSpec((rows, n_cols), lambda r: (r, 0)),
        out_shape=jax.ShapeDtypeStruct((n_rows, n_cols), BF16),
        compiler_params=pltpu.CompilerParams(
            dimension_semantics=("arbitrary",),
            vmem_limit_bytes=_vmem_limit(blk)),
        name="cast_weights",
    )(src)


def _ep_swiglu(accs, j, e_refs, o_refs):
    a, b = accs
    o_refs[0][...] = (_silu(a) * b).astype(o_refs[0].dtype)


def _ep_residual(accs, j, e_refs, o_refs, *, coef):
    h_ref, gate_ref = e_refs
    o_refs[0][...] = h_ref[...] + (coef * gate_ref[0]) * accs[0]


def _ep_qkv(accs, j, e_refs, o_refs, *, n_norm_tiles):
    y = accs[0]
    gain_ref, = e_refs
    o_ref = o_refs[0]
    tn = y.shape[1]

    @pl.when(j < n_norm_tiles)
    def _():
        g = gain_ref[0]
        for hd in range(tn // HEAD_DIM):
            yh = y[:, hd * HEAD_DIM:(hd + 1) * HEAD_DIM]
            inv = lax.rsqrt(jnp.mean(yh * yh, axis=-1, keepdims=True) + EPS)
            o_ref[:, hd * HEAD_DIM:(hd + 1) * HEAD_DIM] = (yh * inv * g).astype(o_ref.dtype)

    @pl.when(j >= n_norm_tiles)
    def _():
        o_ref[...] = y.astype(o_ref.dtype)


def _ep_silu(accs, j, e_refs, o_refs):
    o_refs[0][...] = _silu(accs[0]).astype(o_refs[0].dtype)


def _ep_cast(accs, j, e_refs, o_refs):
    o_refs[0][...] = accs[0].astype(o_refs[0].dtype)


def _ep_forget(accs, j, e_refs, o_refs, *, layer):
    f = accs[0]
    lbp = e_refs[0][...]
    depth = lbp.shape[0]
    mx = lbp[0:1]
    for l in range(1, depth):
        mx = jnp.maximum(mx, lbp[l:l + 1])
    es = [jnp.exp(lbp[l:l + 1] - mx) for l in range(depth)]
    tot = es[0]
    for l in range(1, depth):
        tot = tot + es[l]
    s = [e / tot for e in es]
    cum = s[0]
    for l in range(1, layer + 1):
        cum = cum + s[l]
    lb = cum - s[0]
    a2 = jnp.log(lb) * LOG2_E
    c2 = jnp.log1p(-lb) * LOG2_E
    f2 = f * LOG2_E
    t = jnp.exp2(-jnp.abs(f2))
    one_t = 1.0 + t
    b2 = (c2 + jnp.minimum(f2, 0.0)) - jnp.log(one_t) * LOG2_E
    e = jnp.exp2(-jnp.abs(a2 - b2))
    o_refs[0][...] = jnp.maximum(a2, b2) + jnp.log(1.0 + e) * LOG2_E
    o_refs[1][...] = ((1.0 - lb) * jnp.where(f >= 0.0, t, 1.0) / one_t).astype(o_refs[1].dtype)


def _split3(x):
    hi = x.astype(BF16)
    r = x - hi.astype(F32)
    mid = r.astype(BF16)
    lo = (r - mid.astype(F32)).astype(BF16)
    return hi, mid, lo


def _attn_tables(sinks, *, n_kv, group, blk):
    n_heads = n_kv * group
    q_feat = jnp.zeros((n_heads, blk, LANES), BF16)
    for i, s_i in enumerate(_split3(sinks)):
        q_feat = q_feat.at[:, :, i].set(jnp.broadcast_to(s_i[:, None], (n_heads, blk)))
    q_feat = q_feat.reshape(n_kv, group * blk, LANES)
    k_feat = np.zeros((2 * blk, LANES), np.float32)
    k_feat[0, 0:3] = 1.0
    qi = np.arange(blk)[:, None]
    kj = np.arange(2 * blk)[None, :]
    dist = (qi + blk - kj).astype(np.float32)
    valid = (dist >= 0) & (dist < WINDOW)
    sink_col = kj == 0
    inf = np.float32(np.inf)
    dist_all = np.where(sink_col, np.float32(0), np.where(valid, dist, inf))
    dist_first = np.where(sink_col, np.float32(0), np.where(valid & (kj >= blk), dist, inf))
    return q_feat, jnp.asarray(k_feat, BF16), jnp.asarray(dist_all), jnp.asarray(dist_first)


def _attn_kernel(slope_ref, qf_ref, kf_ref, dist_ref, dist_first_ref, q_ref, kc_ref, kp_ref, vc_ref,
                 vp_ref, o_ref, d_scr, *, n_kv, group):
    blk = q_ref.shape[0]
    d_scr[...] = jnp.where(pl.program_id(1) > 0, dist_ref[...], dist_first_ref[...])
    keep = (lax.broadcasted_iota(jnp.int32, (16, HEAD_DIM), 0) != 0).astype(BF16)
    ones = jnp.ones((2 * blk, HEAD_DIM), BF16)
    k_feat = kf_ref[...]

    def kv_head(hk, carry):
        ko = pl.ds(pl.multiple_of(hk * HEAD_DIM, HEAD_DIM), HEAD_DIM)
        kp = kp_ref[:, ko]
        vp = vp_ref[:, ko]
        vp = jnp.concatenate([vp[:16] * keep, vp[16:]], axis=0)
        kp = jnp.concatenate([kp[:16] * keep, kp[16:]], axis=0)
        k2 = jnp.concatenate([jnp.concatenate([kp, kc_ref[:, ko]], axis=0), k_feat], axis=1)
        v2 = jnp.concatenate([jnp.concatenate([vp, vc_ref[:, ko]], axis=0), ones], axis=1)
        cols = [pl.ds(pl.multiple_of((hk * group + g) * HEAD_DIM, HEAD_DIM), HEAD_DIM)
                for g in range(group)]
        q4 = jnp.concatenate([q_ref[:, cg] for cg in cols], axis=0)
        q4 = jnp.concatenate([q4, qf_ref[hk]], axis=1)
        s = _dot_nt(q4, k2)
        dist = d_scr[...]
        logits = jnp.concatenate(
            [s[g * blk:(g + 1) * blk] - slope_ref[hk * group + g] * dist for g in range(group)],
            axis=0)
        m = jnp.max(logits, axis=-1, keepdims=True)
        p = jnp.exp(logits - m).astype(BF16)
        o = _dot(p, v2)
        out = o[:, :HEAD_DIM] / o[:, HEAD_DIM:]
        for g, cg in enumerate(cols):
            o_ref[:, cg] = out[g * blk:(g + 1) * blk].astype(o_ref.dtype)
        return carry

    lax.fori_loop(0, n_kv, kv_head, 0, unroll=True)


def _attention(qkv, slopes, sinks, *, batch, seq, n_heads, n_kv):
    m = qkv.shape[0]
    blk = WINDOW
    group = n_heads // n_kv
    nb = seq // blk
    dq = n_heads * HEAD_DIM
    dkv = n_kv * HEAD_DIM
    k_col = dq // dkv
    v_col = k_col + 1
    q_feat, k_feat, dist, dist_first = _attn_tables(sinks, n_kv=n_kv, group=group, blk=blk)
    cur = lambda b, n: b * nb + n
    prev = lambda b, n: b * nb + jnp.maximum(n - 1, 0)
    const2 = lambda b, n: (0, 0)
    blk_bytes = (2 * _nbytes((blk, dq), BF16) + 4 * _nbytes((blk, dkv), BF16)
                 + _nbytes(q_feat.shape, BF16) + 3 * _nbytes(dist.shape, F32))
    return pl.pallas_call(
        functools.partial(_attn_kernel, n_kv=n_kv, group=group),
        grid=(batch, nb),
        in_specs=[
            pl.BlockSpec(memory_space=pltpu.SMEM),
            pl.BlockSpec(q_feat.shape, lambda b, n: (0, 0, 0)),
            pl.BlockSpec(k_feat.shape, const2),
            pl.BlockSpec(dist.shape, const2),
            pl.BlockSpec(dist.shape, const2),
            pl.BlockSpec((blk, dq), lambda b, n: (cur(b, n), 0)),
            pl.BlockSpec((blk, dkv), lambda b, n: (cur(b, n), k_col)),
            pl.BlockSpec((blk, dkv), lambda b, n: (prev(b, n), k_col)),
            pl.BlockSpec((blk, dkv), lambda b, n: (cur(b, n), v_col)),
            pl.BlockSpec((blk, dkv), lambda b, n: (prev(b, n), v_col)),
        ],
        out_specs=pl.BlockSpec((blk, dq), lambda b, n: (cur(b, n), 0)),
        out_shape=jax.ShapeDtypeStruct((m, dq), BF16),
        scratch_shapes=[pltpu.VMEM(dist.shape, F32)],
        compiler_params=pltpu.CompilerParams(
            dimension_semantics=("arbitrary", "arbitrary"),
            vmem_limit_bytes=_vmem_limit(blk_bytes)),
        name="swa_attention",
    )(slopes, q_feat, k_feat, dist, dist_first, qkv, qkv, qkv, qkv, qkv)


def _hgrn_tables(c):
    halves = [c >> (lv + 1) for lv in range(int(np.log2(c)))]
    r = np.arange(c)[:, None]
    j = np.arange(c)[None, :]

    def partial_sums(half):
        ref = (r // (2 * half)) * (2 * half) + half - 1
        return np.where(r > ref, (j > ref) & (j <= r), (j > r) & (j <= ref))

    masks = [((r // (2 * h)) == (j // (2 * h))) & ((r % (2 * h)) >= h) & ((j % (2 * h)) < h)
             for h in halves]
    masks.append(r == j)
    w = np.concatenate([j <= r, partial_sums(4), partial_sums(2)], axis=0).astype(np.float32)
    w3 = np.concatenate([w, w, w], axis=1)
    return jnp.asarray(w3, BF16), jnp.asarray(np.stack(masks).astype(np.float32), BF16), halves


def _hgrn_kernel(w3_ref, mask_ref, gain_ref, q_ref, k_ref, v_ref, lf_ref, gate_ref, o_ref,
                 s_ref, d_scr, *, c, halves):
    @pl.when(pl.program_id(2) == 0)
    def _():
        s_ref[...] = jnp.zeros_like(s_ref)

    n_chunks = q_ref.shape[0] // c
    zero_h = jnp.zeros((c, LANES), BF16)
    n_lv = len(halves)

    def block_diag(x0, x1):
        return jnp.concatenate([jnp.concatenate([x0, zero_h], axis=1),
                                jnp.concatenate([zero_h, x1], axis=1)], axis=0)

    def both(x):
        return x[:, :LANES], x[:, LANES:]

    def pair_mask(i):
        mk = mask_ref[i]
        return jnp.concatenate([mk, mk], axis=1)

    def chunk(ci, carry):
        rows = pl.ds(pl.multiple_of(ci * c, c), c)
        q = lambda: q_ref[rows, :]
        k = lambda: k_ref[rows, :]
        lf3 = jnp.concatenate(_split3(lf_ref[rows, :]), axis=0)
        d_scr[...] = _dot(w3_ref[...], lf3)
        b = lambda: d_scr[0:c, :]

        def level_exponent(half):
            pieces = []
            for g0 in range(0, c, 2 * half):
                ref = jnp.broadcast_to(d_scr[g0 + half - 1:g0 + half, :], (half, 2 * LANES))
                pieces += [ref - d_scr[g0:g0 + half, :], d_scr[g0 + half:g0 + 2 * half, :] - ref]
            return jnp.concatenate(pieces, axis=0)

        e1 = jnp.exp2(lf_ref[rows, :]).astype(BF16)
        p = _dot_nt(jnp.concatenate([q(), q() * e1], axis=0), block_diag(*both(k())))
        a = p[0:c].astype(BF16) * pair_mask(n_lv) + p[c:2 * c].astype(BF16) * pair_mask(n_lv - 1)
        for li, half in enumerate(halves[:-1]):
            if half >= 8:
                dn = level_exponent(half)
            else:
                dn = d_scr[c:2 * c, :] if half == 4 else d_scr[2 * c:3 * c, :]
            e = jnp.exp2(dn).astype(BF16)
            a = a + _dot_nt(q() * e, block_diag(*both(k() * e))).astype(BF16) * pair_mask(li)
        intra = _dot(a, block_diag(*both(v_ref[rows, :])))

        s0 = s_ref[0]
        s1 = s_ref[1]
        inter = _dot_nt(q() * jnp.exp2(b()).astype(BF16),
                        block_diag(s0.astype(BF16), s1.astype(BF16)))
        o = inter + intra

        b_last = d_scr[c - 1:c, :]
        u = _dot_tn(v_ref[rows, :], k() * jnp.exp2(b_last - b()).astype(BF16))
        decay = jnp.exp2(b_last)
        s_ref[0] = s0 * decay[:, :LANES] + u[:LANES, :LANES]
        s_ref[1] = s1 * decay[:, LANES:] + u[LANES:, LANES:]

        gate = gate_ref[rows, :].astype(F32)
        gain = gain_ref[...]
        for hh in range(2):
            lanes = slice(hh * LANES, (hh + 1) * LANES)
            oh = o[:, lanes]
            y = oh * lax.rsqrt(jnp.mean(oh * oh, axis=-1, keepdims=True) + EPS) * gain
            o_ref[rows, lanes] = (y * gate[:, lanes]).astype(o_ref.dtype)
        return carry

    lax.fori_loop(0, n_chunks, chunk, 0, unroll=min(n_chunks, 8))


def _hgrn_recurrence(q, k, v, log_f, gate, gain, *, batch, seq, tb=2048):
    m, d = q.shape
    c = HGRN_CHUNK
    w3, masks, halves = _hgrn_tables(c)
    tb = min(tb, seq)
    nt = seq // tb
    width = 2 * HEAD_DIM
    tile = pl.BlockSpec((tb, width), lambda b, h, t: (b * nt + t, h))
    const2 = lambda b, h, t: (0, 0)
    blk = (_nbytes((tb, width), F32) + 5 * _nbytes((tb, width), BF16)
           + _nbytes(w3.shape, BF16) + _nbytes(masks.shape, BF16))
    return pl.pallas_call(
        functools.partial(_hgrn_kernel, c=c, halves=halves),
        grid=(batch, d // width, nt),
        in_specs=[
            pl.BlockSpec(w3.shape, const2),
            pl.BlockSpec(masks.shape, lambda b, h, t: (0, 0, 0)),
            pl.BlockSpec((1, HEAD_DIM), const2),
            tile, tile, tile, tile, tile,
        ],
        out_specs=tile,
        out_shape=jax.ShapeDtypeStruct((m, d), BF16),
        scratch_shapes=[pltpu.VMEM((2, HEAD_DIM, HEAD_DIM), F32), pltpu.VMEM((3 * c, width), F32)],
        compiler_params=pltpu.CompilerParams(
            dimension_semantics=("arbitrary", "arbitrary", "arbitrary"),
            vmem_limit_bytes=_vmem_limit(blk)),
        name="hgrn2_recurrence",
    )(w3, masks, gain.reshape(1, HEAD_DIM), q, k, v, log_f, gate)


def _bcast_mod(mod_l, idx, batch):
    d = mod_l.shape[-1] // N_MOD
    return mod_l[:batch, idx * d:(idx + 1) * d].reshape(batch, 1, d)


def _gate_up(u, w_gu, cast):
    f = w_gu.shape[1] // 2
    tn = 512
    return _mm(u, w_gu, [0, f // tn], f, [], [BF16], _ep_swiglu, name="ffn_gate_up",
               cast=cast, tm=1024, tn=tn)


def _residual_mm(x, w, h, gate, coef, cast, *, rows_per_batch, name):
    tm, tn = 1024, 1024
    tpb = rows_per_batch // tm
    extras = [
        (h, (tm, tn), lambda i, j, k: (i, j)),
        (gate, (1, 1, tn), lambda i, j, k: (i // tpb, 0, j)),
    ]
    return _mm(x, w, [0], w.shape[1], extras, [F32], functools.partial(_ep_residual, coef=coef),
               name=name, cast=cast, tm=tm, tn=tn, tk=2048)


def kernel(x, c, ada_w, ada_b, norm_gains, ffn1_w_gu, ffn1_w_down, ffn2_w_gu, ffn2_w_down,
           attn_w_qkv, attn_w_o, attn_q_gain, attn_k_gain, attn_sinks,
           hgrn_w_qfig, hgrn_w_o, hgrn_g_gain, hgrn_lower_bounds):
    batch, seq, d = x.shape
    depth = ada_w.shape[0]
    m = batch * seq
    n_mixers = 2

    c_pad = jnp.zeros((8, d), F32).at[:batch].set(c)
    mod = _ada(c_pad, ada_w, ada_b)

    w_gu = _cast_bf16(ffn1_w_gu, 0)
    h = x.reshape(m, d)
    for i in range(depth):
        md = lambda idx: _bcast_mod(mod[i], idx, batch)
        jm = i // n_mixers
        is_attn = i % n_mixers == 0
        u = _norm_mod(h, norm_gains[i, 0], md(0), md(1), rows_per_batch=seq)
        act, w_down = _gate_up(u, w_gu, (ffn1_w_down, i))
        h, w_mix = _residual_mm(act, w_down, h, md(2), 0.5,
                                (attn_w_qkv, jm) if is_attn else (hgrn_w_qfig, jm),
                                rows_per_batch=seq, name="ffn_down")
        u = _norm_mod(h, norm_gains[i, 1], md(3), md(4), rows_per_batch=seq)
        if is_attn:
            n_heads = attn_sinks.shape[1]
            n_qkv = w_mix.shape[1]
            n_kv = (n_qkv // HEAD_DIM - n_heads) // 2
            tn = n_kv * HEAD_DIM
            q_tiles = n_heads // n_kv
            ones = jnp.ones((1, HEAD_DIM), F32)
            gains = jnp.stack([attn_q_gain[jm].reshape(1, HEAD_DIM) * HEAD_DIM ** -0.5] * q_tiles
                              + [attn_k_gain[jm].reshape(1, HEAD_DIM), ones])
            qkv, w_o = _mm(u, w_mix, [0], n_qkv,
                           [(gains, (1, 1, HEAD_DIM), lambda i_, j_, k_: (j_, 0, 0))], [BF16],
                           functools.partial(_ep_qkv, n_norm_tiles=q_tiles + 1),
                           name="attn_qkv", cast=(attn_w_o, jm), tm=1024, tn=tn)
            heads = jnp.arange(1, n_heads + 1, dtype=F32)
            slopes = jnp.exp2(-8.0 * heads / n_heads)
            y = _attention(qkv, slopes, attn_sinks[jm].astype(F32), batch=batch, seq=seq,
                           n_heads=n_heads, n_kv=n_kv)
            mixer_name = "attn_out"
        else:
            fd = (w_mix.shape[1] - 2 * d) // 2
            tn = 1024
            qh, w_o = _mm(u, w_mix, [0], fd, [], [BF16], _ep_silu, name="hgrn_q",
                          cast=(hgrn_w_o, jm), tn=tn)
            lbp = hgrn_lower_bounds.astype(F32)
            log_f, kh = _mm(u, w_mix, [fd // tn], fd,
                            [(lbp, (depth, tn), lambda i_, j_, k_: (0, j_))], [F32, BF16],
                            functools.partial(_ep_forget, layer=i), name="hgrn_f", tn=tn)
            vh, = _mm(u, w_mix, [2 * fd // tn], d, [], [BF16], _ep_cast, name="hgrn_i", tn=tn)
            gh, = _mm(u, w_mix, [(2 * fd + d) // tn], d, [], [BF16], _ep_silu, name="hgrn_g", tn=tn)
            y = _hgrn_recurrence(qh, kh, vh, log_f, gh, hgrn_g_gain[jm], batch=batch, seq=seq)
            mixer_name = "hgrn_out"
        h, w_gu = _residual_mm(y, w_o, h, md(5), 1.0, (ffn2_w_gu, i), rows_per_batch=seq,
                               name=mixer_name)
        u = _norm_mod(h, norm_gains[i, 2], md(6), md(7), rows_per_batch=seq)
        act, w_down = _gate_up(u, w_gu, (ffn2_w_down, i))
        if i + 1 < depth:
            h, w_gu = _residual_mm(act, w_down, h, md(8), 0.5, (ffn1_w_gu, i + 1),
                                   rows_per_batch=seq, name="ffn_down")
        else:
            h, = _residual_mm(act, w_down, h, md(8), 0.5, None, rows_per_batch=seq, name="ffn_down")
    return h.reshape(batch, seq, d)
```

```python
import functools

import numpy as np
import jax
import jax.numpy as jnp
from jax import lax
from jax.experimental import pallas as pl
from jax.experimental.pallas import tpu as pltpu

F32 = jnp.float32
BF16 = jnp.bfloat16
EPS = 1e-6
LOG2_E = 1.4426950408889634

WINDOW = 128
HEAD_DIM = 128
N_MOD = 9
HGRN_CHUNK = 128

V7X_VMEM_BYTES = 64 * 1024 * 1024
VMEM_HEADROOM_BYTES = 6 * 1024 * 1024
LANES = 128


def _vmem_limit(block_bytes):
    want = 2 * block_bytes + VMEM_HEADROOM_BYTES
    return int(min(max(want, 32 * 1024 * 1024), V7X_VMEM_BYTES - 2 * 1024 * 1024))


def _nbytes(shape, dtype):
    return int(np.prod(shape)) * jnp.dtype(dtype).itemsize


def _silu(x):
    return x * jax.nn.sigmoid(x)


def _dot(a, b):
    return jnp.dot(a, b, preferred_element_type=F32)


def _dot_nt(a, b):
    return lax.dot_general(a, b, (((1,), (1,)), ((), ())), preferred_element_type=F32)


def _dot_tn(a, b):
    return lax.dot_general(a, b, (((0,), (0,)), ((), ())), preferred_element_type=F32)


def _ada_kernel(c_ref, w_ref, b_ref, o_ref):
    cond = _silu(c_ref[...]).astype(BF16)
    o_ref[0] = _dot(cond, w_ref[0].astype(BF16)) + b_ref[0]


def _ada(c_pad, ada_w, ada_b, *, tn=512):
    depth, d, nd = ada_w.shape
    rows = c_pad.shape[0]
    blk = _nbytes((d, tn), F32) + _nbytes((rows, tn), F32) + _nbytes((rows, d), F32)
    return pl.pallas_call(
        _ada_kernel,
        grid=(depth, nd // tn),
        in_specs=[
            pl.BlockSpec((rows, d), lambda l, j: (0, 0)),
            pl.BlockSpec((1, d, tn), lambda l, j: (l, 0, j)),
            pl.BlockSpec((1, 1, tn), lambda l, j: (l, 0, j)),
        ],
        out_specs=pl.BlockSpec((1, rows, tn), lambda l, j: (l, 0, j)),
        out_shape=jax.ShapeDtypeStruct((depth, rows, nd), F32),
        compiler_params=pltpu.CompilerParams(
            dimension_semantics=("arbitrary", "arbitrary"),
            vmem_limit_bytes=_vmem_limit(blk)),
        name="ada_mod",
    )(c_pad, ada_w, ada_b.reshape(depth, 1, nd))


def _row_stats_kernel(x_ref, c1_ref, xc_ref, ssq_ref):
    x = x_ref[...]
    xc_ref[...] = (x * c1_ref[0]).astype(xc_ref.dtype)
    ssq_ref[...] = jnp.broadcast_to(jnp.sum(x * x, axis=-1, keepdims=True), ssq_ref.shape)


def _row_stats(x, c1, *, rows_per_batch, tm=512):
    m, d = x.shape
    tpb = rows_per_batch // tm
    blk = _nbytes((tm, d), F32) + _nbytes((tm, d), BF16) + _nbytes((tm, LANES), F32)
    return pl.pallas_call(
        _row_stats_kernel,
        grid=(m // tm,),
        in_specs=[pl.BlockSpec((tm, d), lambda i: (i, 0)),
                  pl.BlockSpec((1, 1, d), lambda i: (i // tpb, 0, 0))],
        out_specs=[pl.BlockSpec((tm, d), lambda i: (i, 0)), pl.BlockSpec((tm, LANES), lambda i: (i, 0))],
        out_shape=[jax.ShapeDtypeStruct((m, d), BF16), jax.ShapeDtypeStruct((m, LANES), F32)],
        compiler_params=pltpu.CompilerParams(
            dimension_semantics=("arbitrary",),
            vmem_limit_bytes=_vmem_limit(blk)),
        name="row_stats",
    )(x, c1)


def _mm_kernel(*refs, n_w, n_e, n_o, nk, has_norm, has_cast, epilogue):
    x_ref = refs[0]
    w_refs = refs[1:1 + n_w]
    e_refs = refs[1 + n_w:1 + n_w + n_e]
    n_in = 1 + n_w + n_e + 2 * has_norm + has_cast
    o_refs = refs[n_in:n_in + n_o]
    scratch = refs[n_in + n_o + has_cast:]
    j = pl.program_id(1)

    if has_cast:
        cast_out = refs[n_in + n_o]
        cast_out[...] = refs[n_in - 1][...].astype(cast_out.dtype)

    if has_norm:
        ssq_ref, shift_ref = refs[1 + n_w + n_e:3 + n_w + n_e]
        xn_ref, scratch = scratch[0], scratch[1:]

        @pl.when(j == 0)
        def _():
            r = lax.rsqrt(ssq_ref[...] * (1.0 / x_ref.shape[1]) + EPS)
            r = jnp.concatenate([r] * (x_ref.shape[1] // LANES), axis=1)
            xn_ref[...] = (x_ref[...].astype(F32) * r + shift_ref[...]).astype(xn_ref.dtype)

        x_ref = xn_ref
    acc_refs = scratch

    def dots():
        x = x_ref[...]
        return [_dot(x, w[...]) for w in w_refs]

    if nk == 1:
        epilogue(dots(), j, e_refs, o_refs)
        return

    k = pl.program_id(2)

    @pl.when(k == 0)
    def _():
        for a, d in zip(acc_refs, dots()):
            a[...] = d

    if nk > 2:
        @pl.when((k > 0) & (k < nk - 1))
        def _():
            for a, d in zip(acc_refs, dots()):
                a[...] += d

    @pl.when(k == nk - 1)
    def _():
        epilogue([a[...] + d for a, d in zip(acc_refs, dots())], j, e_refs, o_refs)


def _cast_chunk_rows(n_rows, n_steps):
    rows = 16
    while n_rows % rows or n_rows // rows > n_steps:
        rows += 16
    return rows


def _mm(x, w, w_col_blocks, n_cols, extras, outs, epilogue, *, name, norm=None, cast=None,
        rows_per_batch=None, tm=1024, tn=1024, tk=4096):
    m, kdim = x.shape
    tk = min(tk, kdim)
    nk = kdim // tk
    grid = (m // tm, n_cols // tn, nk)
    in_specs = [pl.BlockSpec((tm, tk), lambda i, j, k: (i, k))]
    blk = _nbytes((tm, tk), x.dtype)
    for off in w_col_blocks:
        in_specs.append(pl.BlockSpec((tk, tn), lambda i, j, k, off=off: (k, j + off)))
        blk += _nbytes((tk, tn), w.dtype)
    for arr, bshape, imap in extras:
        in_specs.append(pl.BlockSpec(bshape, imap))
        blk += _nbytes(bshape, arr.dtype)
    operands = [x, *([w] * len(w_col_blocks)), *[e[0] for e in extras]]
    scratch = []
    if norm is not None:
        assert nk == 1
        tpb = rows_per_batch // tm
        in_specs += [pl.BlockSpec((tm, LANES), lambda i, j, k: (i, 0)),
                     pl.BlockSpec((None, 1, kdim), lambda i, j, k: (i // tpb, 0, 0))]
        operands += list(norm)
        scratch.append(pltpu.VMEM((tm, kdim), x.dtype))
        blk += _nbytes((tm, LANES), F32) + _nbytes((tm, kdim), x.dtype) // 2
    out_specs, out_shape = [], []
    for dt in outs:
        if dt == "ssq":
            out_specs.append(pl.BlockSpec((tm, LANES), lambda i, j, k: (i, 0)))
            out_shape.append(jax.ShapeDtypeStruct((m, LANES), F32))
            blk += _nbytes((tm, LANES), F32)
        else:
            out_specs.append(pl.BlockSpec((tm, tn), lambda i, j, k: (i, j)))
            out_shape.append(jax.ShapeDtypeStruct((m, n_cols), dt))
            blk += _nbytes((tm, tn), dt)
    if cast is not None:
        src, layer = cast
        _, n_rows, n_src_cols = src.shape
        rows = _cast_chunk_rows(n_rows, grid[0] * grid[1] * grid[2])
        last = n_rows // rows - 1
        chunk = lambda i, j, k: jnp.minimum((i * grid[1] + j) * grid[2] + k, last)
        in_specs.append(pl.BlockSpec((None, rows, n_src_cols), lambda i, j, k: (layer, chunk(i, j, k), 0)))
        out_specs.append(pl.BlockSpec((rows, n_src_cols), lambda i, j, k: (chunk(i, j, k), 0)))
        out_shape.append(jax.ShapeDtypeStruct((n_rows, n_src_cols), BF16))
        operands.append(src)
        blk += _nbytes((rows, n_src_cols), F32) + _nbytes((rows, n_src_cols), BF16)
    n_w = len(w_col_blocks)
    if nk > 1:
        scratch += [pltpu.VMEM((tm, tn), F32) for _ in range(n_w)]
    scratch_bytes = ((n_w if nk > 1 else 0) + 3 * n_w) * _nbytes((tm, tn), F32)
    kern = functools.partial(_mm_kernel, n_w=n_w, n_e=len(extras), n_o=len(outs), nk=nk,
                             has_norm=int(norm is not None), has_cast=int(cast is not None),
                             epilogue=epilogue)
    return pl.pallas_call(
        kern,
        grid=grid,
        in_specs=in_specs,
        out_specs=out_specs,
        out_shape=out_shape,
        scratch_shapes=scratch,
        compiler_params=pltpu.CompilerParams(
            dimension_semantics=("arbitrary", "arbitrary", "arbitrary"),
            vmem_limit_bytes=_vmem_limit(blk + scratch_bytes // 2)),
        name=name,
    )(*operands)


def _cast_kernel(src_ref, o_ref):
    o_ref[...] = src_ref[...].astype(o_ref.dtype)


def _cast_bf16(src, layer, *, rows=128):
    _, n_rows, n_cols = src.shape
    blk = _nbytes((rows, n_cols), F32) + _nbytes((rows, n_cols), BF16)
    return pl.pallas_call(
        _cast_kernel,
        grid=(n_rows // rows,),
        in_specs=[pl.BlockSpec((None, rows, n_cols), lambda r: (layer, r, 0))],
        out_specs=pl.BlockSpec((rows, n_cols), lambda r: (r, 0)),
        out_shape=jax.ShapeDtypeStruct((n_rows, n_cols), BF16),
        compiler_params=pltpu.CompilerParams(
            dimension_semantics=("arbitrary",),
            vmem_limit_bytes=_vmem_limit(blk)),
        name="cast_weights",
    )(src)


def _ep_swiglu(accs, j, e_refs, o_refs):
    a, b = accs
    o_refs[0][...] = (_silu(a) * b).astype(o_refs[0].dtype)


def _ep_residual(accs, j, e_refs, o_refs, *, coef):
    h_ref, gate_ref = e_refs[:2]
    hn = h_ref[...] + (coef * gate_ref[0]) * accs[0]
    o_refs[0][...] = hn
    if len(o_refs) == 1:
        return
    o_refs[1][...] = (hn * e_refs[2][0]).astype(o_refs[1].dtype)
    ssq_ref = o_refs[2]
    part = jnp.broadcast_to(jnp.sum(hn * hn, axis=-1, keepdims=True), ssq_ref.shape)

    @pl.when(j == 0)
    def _():
        ssq_ref[...] = part

    @pl.when(j > 0)
    def _():
        ssq_ref[...] += part


def _ep_qkv(accs, j, e_refs, o_refs, *, n_norm_tiles):
    y = accs[0]
    gain_ref, = e_refs
    o_ref = o_refs[0]
    tn = y.shape[1]

    @pl.when(j < n_norm_tiles)
    def _():
        g = gain_ref[0]
        for hd in range(tn // HEAD_DIM):
            yh = y[:, hd * HEAD_DIM:(hd + 1) * HEAD_DIM]
            inv = lax.rsqrt(jnp.mean(yh * yh, axis=-1, keepdims=True) + EPS)
            o_ref[:, hd * HEAD_DIM:(hd + 1) * HEAD_DIM] = (yh * inv * g).astype(o_ref.dtype)

    @pl.when(j >= n_norm_tiles)
    def _():
        o_ref[...] = y.astype(o_ref.dtype)


def _ep_silu(accs, j, e_refs, o_refs):
    o_refs[0][...] = _silu(accs[0]).astype(o_refs[0].dtype)


def _ep_cast(accs, j, e_refs, o_refs):
    o_refs[0][...] = accs[0].astype(o_refs[0].dtype)


def _ep_forget(accs, j, e_refs, o_refs, *, layer):
    f = accs[0]
    lbp = e_refs[0][...]
    depth = lbp.shape[0]
    mx = lbp[0:1]
    for l in range(1, depth):
        mx = jnp.maximum(mx, lbp[l:l + 1])
    es = [jnp.exp(lbp[l:l + 1] - mx) for l in range(depth)]
    tot = es[0]
    for l in range(1, depth):
        tot = tot + es[l]
    s = [e / tot for e in es]
    cum = s[0]
    for l in range(1, layer + 1):
        cum = cum + s[l]
    lb = cum - s[0]
    a2 = jnp.log(lb) * LOG2_E
    c2 = jnp.log1p(-lb) * LOG2_E
    f2 = f * LOG2_E
    t = jnp.exp2(-jnp.abs(f2))
    one_t = 1.0 + t
    b2 = (c2 + jnp.minimum(f2, 0.0)) - jnp.log(one_t) * LOG2_E
    e = jnp.exp2(-jnp.abs(a2 - b2))
    o_refs[0][...] = jnp.maximum(a2, b2) + jnp.log(1.0 + e) * LOG2_E
    o_refs[1][...] = ((1.0 - lb) * jnp.where(f >= 0.0, t, 1.0) / one_t).astype(o_refs[1].dtype)


def _split3(x):
    hi = x.astype(BF16)
    r = x - hi.astype(F32)
    mid = r.astype(BF16)
    lo = (r - mid.astype(F32)).astype(BF16)
    return hi, mid, lo


def _attn_tables(sinks, *, n_kv, group, blk):
    n_heads = n_kv * group
    q_feat = jnp.zeros((n_heads, blk, LANES), BF16)
    for i, s_i in enumerate(_split3(sinks)):
        q_feat = q_feat.at[:, :, i].set(jnp.broadcast_to(s_i[:, None], (n_heads, blk)))
    q_feat = q_feat.reshape(n_kv, group * blk, LANES)
    k_feat = np.zeros((2 * blk, LANES), np.float32)
    k_feat[0, 0:3] = 1.0
    qi = np.arange(blk)[:, None]
    kj = np.arange(2 * blk)[None, :]
    dist = (qi + blk - kj).astype(np.float32)
    valid = (dist >= 0) & (dist < WINDOW)
    sink_col = kj == 0
    inf = np.float32(np.inf)
    dist_all = np.where(sink_col, np.float32(0), np.where(valid, dist, inf))
    dist_first = np.where(sink_col, np.float32(0), np.where(valid & (kj >= blk), dist, inf))
    return q_feat, jnp.asarray(k_feat, BF16), jnp.asarray(dist_all), jnp.asarray(dist_first)


def _attn_kernel(slope_ref, qf_ref, kf_ref, dist_ref, dist_first_ref, q_ref, kc_ref, kp_ref, vc_ref,
                 vp_ref, o_ref, d_scr, *, n_kv, group):
    blk = q_ref.shape[0]
    d_scr[...] = jnp.where(pl.program_id(1) > 0, dist_ref[...], dist_first_ref[...])
    keep = (lax.broadcasted_iota(jnp.int32, (16, HEAD_DIM), 0) != 0).astype(BF16)
    ones = jnp.ones((2 * blk, HEAD_DIM), BF16)
    k_feat = kf_ref[...]

    def kv_head(hk, carry):
        ko = pl.ds(pl.multiple_of(hk * HEAD_DIM, HEAD_DIM), HEAD_DIM)
        kp = kp_ref[:, ko]
        vp = vp_ref[:, ko]
        vp = jnp.concatenate([vp[:16] * keep, vp[16:]], axis=0)
        kp = jnp.concatenate([kp[:16] * keep, kp[16:]], axis=0)
        k2 = jnp.concatenate([jnp.concatenate([kp, kc_ref[:, ko]], axis=0), k_feat], axis=1)
        v2 = jnp.concatenate([jnp.concatenate([vp, vc_ref[:, ko]], axis=0), ones], axis=1)
        cols = [pl.ds(pl.multiple_of((hk * group + g) * HEAD_DIM, HEAD_DIM), HEAD_DIM)
                for g in range(group)]
        q4 = jnp.concatenate([q_ref[:, cg] for cg in cols], axis=0)
        q4 = jnp.concatenate([q4, qf_ref[hk]], axis=1)
        s = _dot_nt(q4, k2)
        dist = d_scr[...]
        logits = jnp.concatenate(
            [s[g * blk:(g + 1) * blk] - slope_ref[hk * group + g] * dist for g in range(group)],
            axis=0)
        m = jnp.max(logits, axis=-1, keepdims=True)
        p = jnp.exp(logits - m).astype(BF16)
        o = _dot(p, v2)
        out = o[:, :HEAD_DIM] / o[:, HEAD_DIM:]
        for g, cg in enumerate(cols):
            o_ref[:, cg] = out[g * blk:(g + 1) * blk].astype(o_ref.dtype)
        return carry

    lax.fori_loop(0, n_kv, kv_head, 0, unroll=True)


def _attention(qkv, slopes, sinks, *, batch, seq, n_heads, n_kv):
    m = qkv.shape[0]
    blk = WINDOW
    group = n_heads // n_kv
    nb = seq // blk
    dq = n_heads * HEAD_DIM
    dkv = n_kv * HEAD_DIM
    k_col = dq // dkv
    v_col = k_col + 1
    q_feat, k_feat, dist, dist_first = _attn_tables(sinks, n_kv=n_kv, group=group, blk=blk)
    cur = lambda b, n: b * nb + n
    prev = lambda b, n: b * nb + jnp.maximum(n - 1, 0)
    const2 = lambda b, n: (0, 0)
    blk_bytes = (2 * _nbytes((blk, dq), BF16) + 4 * _nbytes((blk, dkv), BF16)
                 + _nbytes(q_feat.shape, BF16) + 3 * _nbytes(dist.shape, F32))
    return pl.pallas_call(
        functools.partial(_attn_kernel, n_kv=n_kv, group=group),
        grid=(batch, nb),
        in_specs=[
            pl.BlockSpec(memory_space=pltpu.SMEM),
            pl.BlockSpec(q_feat.shape, lambda b, n: (0, 0, 0)),
            pl.BlockSpec(k_feat.shape, const2),
            pl.BlockSpec(dist.shape, const2),
            pl.BlockSpec(dist.shape, const2),
            pl.BlockSpec((blk, dq), lambda b, n: (cur(b, n), 0)),
            pl.BlockSpec((blk, dkv), lambda b, n: (cur(b, n), k_col)),
            pl.BlockSpec((blk, dkv), lambda b, n: (prev(b, n), k_col)),
            pl.BlockSpec((blk, dkv), lambda b, n: (cur(b, n), v_col)),
            pl.BlockSpec((blk, dkv), lambda b, n: (prev(b, n), v_col)),
        ],
        out_specs=pl.BlockSpec((blk, dq), lambda b, n: (cur(b, n), 0)),
        out_shape=jax.ShapeDtypeStruct((m, dq), BF16),
        scratch_shapes=[pltpu.VMEM(dist.shape, F32)],
        compiler_params=pltpu.CompilerParams(
            dimension_semantics=("arbitrary", "arbitrary"),
            vmem_limit_bytes=_vmem_limit(blk_bytes)),
        name="swa_attention",
    )(slopes, q_feat, k_feat, dist, dist_first, qkv, qkv, qkv, qkv, qkv)


def _hgrn_tables(c):
    halves = [c >> (lv + 1) for lv in range(int(np.log2(c)))]
    r = np.arange(c)[:, None]
    j = np.arange(c)[None, :]

    def partial_sums(half):
        ref = (r // (2 * half)) * (2 * half) + half - 1
        return np.where(r > ref, (j > ref) & (j <= r), (j > r) & (j <= ref))

    masks = [((r // (2 * h)) == (j // (2 * h))) & ((r % (2 * h)) >= h) & ((j % (2 * h)) < h)
             for h in halves]
    masks.append(r == j)
    w = np.concatenate([j <= r, partial_sums(4), partial_sums(2)], axis=0).astype(np.float32)
    w3 = np.concatenate([w, w, w], axis=1)
    return jnp.asarray(w3, BF16), jnp.asarray(np.stack(masks).astype(np.float32), BF16), halves


def _hgrn_kernel(w3_ref, mask_ref, gain_ref, q_ref, k_ref, v_ref, lf_ref, gate_ref, o_ref,
                 s_ref, d_scr, *, c, halves):
    @pl.when(pl.program_id(2) == 0)
    def _():
        s_ref[...] = jnp.zeros_like(s_ref)

    n_chunks = q_ref.shape[0] // c
    zero_h = jnp.zeros((c, LANES), BF16)
    n_lv = len(halves)

    def block_diag(x0, x1):
        return jnp.concatenate([jnp.concatenate([x0, zero_h], axis=1),
                                jnp.concatenate([zero_h, x1], axis=1)], axis=0)

    def both(x):
        return x[:, :LANES], x[:, LANES:]

    def pair_mask(i):
        mk = mask_ref[i]
        return jnp.concatenate([mk, mk], axis=1)

    def chunk(ci, carry):
        rows = pl.ds(pl.multiple_of(ci * c, c), c)
        q = lambda: q_ref[rows, :]
        k = lambda: k_ref[rows, :]
        lf3 = jnp.concatenate(_split3(lf_ref[rows, :]), axis=0)
        d_scr[...] = _dot(w3_ref[...], lf3)
        b = lambda: d_scr[0:c, :]

        def level_exponent(half):
            pieces = []
            for g0 in range(0, c, 2 * half):
                ref = jnp.broadcast_to(d_scr[g0 + half - 1:g0 + half, :], (half, 2 * LANES))
                pieces += [ref - d_scr[g0:g0 + half, :], d_scr[g0 + half:g0 + 2 * half, :] - ref]
            return jnp.concatenate(pieces, axis=0)

        e1 = jnp.exp2(lf_ref[rows, :]).astype(BF16)
        p = _dot_nt(jnp.concatenate([q(), q() * e1], axis=0), block_diag(*both(k())))
        a = p[0:c].astype(BF16) * pair_mask(n_lv) + p[c:2 * c].astype(BF16) * pair_mask(n_lv - 1)
        for li, half in enumerate(halves[:-1]):
            if half >= 8:
                dn = level_exponent(half)
            else:
                dn = d_scr[c:2 * c, :] if half == 4 else d_scr[2 * c:3 * c, :]
            e = jnp.exp2(dn).astype(BF16)
            a = a + _dot_nt(q() * e, block_diag(*both(k() * e))).astype(BF16) * pair_mask(li)
        intra = _dot(a, block_diag(*both(v_ref[rows, :])))

        s0 = s_ref[0]
        s1 = s_ref[1]
        inter = _dot_nt(q() * jnp.exp2(b()).astype(BF16),
                        block_diag(s0.astype(BF16), s1.astype(BF16)))
        o = inter + intra

        b_last = d_scr[c - 1:c, :]
        u = _dot_tn(v_ref[rows, :], k() * jnp.exp2(b_last - b()).astype(BF16))
        decay = jnp.exp2(b_last)
        s_ref[0] = s0 * decay[:, :LANES] + u[:LANES, :LANES]
        s_ref[1] = s1 * decay[:, LANES:] + u[LANES:, LANES:]

        gate = gate_ref[rows, :].astype(F32)
        gain = gain_ref[...]
        for hh in range(2):
            lanes = slice(hh * LANES, (hh + 1) * LANES)
            oh = o[:, lanes]
            y = oh * lax.rsqrt(jnp.mean(oh * oh, axis=-1, keepdims=True) + EPS) * gain
            o_ref[rows, lanes] = (y * gate[:, lanes]).astype(o_ref.dtype)
        return carry

    lax.fori_loop(0, n_chunks, chunk, 0, unroll=min(n_chunks, 8))


def _hgrn_recurrence(q, k, v, log_f, gate, gain, *, batch, seq, tb=2048):
    m, d = q.shape
    c = HGRN_CHUNK
    w3, masks, halves = _hgrn_tables(c)
    tb = min(tb, seq)
    nt = seq // tb
    width = 2 * HEAD_DIM
    tile = pl.BlockSpec((tb, width), lambda b, h, t: (b * nt + t, h))
    const2 = lambda b, h, t: (0, 0)
    blk = (_nbytes((tb, width), F32) + 5 * _nbytes((tb, width), BF16)
           + _nbytes(w3.shape, BF16) + _nbytes(masks.shape, BF16))
    return pl.pallas_call(
        functools.partial(_hgrn_kernel, c=c, halves=halves),
        grid=(batch, d // width, nt),
        in_specs=[
            pl.BlockSpec(w3.shape, const2),
            pl.BlockSpec(masks.shape, lambda b, h, t: (0, 0, 0)),
            pl.BlockSpec((1, HEAD_DIM), const2),
            tile, tile, tile, tile, tile,
        ],
        out_specs=tile,
        out_shape=jax.ShapeDtypeStruct((m, d), BF16),
        scratch_shapes=[pltpu.VMEM((2, HEAD_DIM, HEAD_DIM), F32), pltpu.VMEM((3 * c, width), F32)],
        compiler_params=pltpu.CompilerParams(
            dimension_semantics=("arbitrary", "arbitrary", "arbitrary"),
            vmem_limit_bytes=_vmem_limit(blk)),
        name="hgrn2_recurrence",
    )(w3, masks, gain.reshape(1, HEAD_DIM), q, k, v, log_f, gate)


def _gate_up(hc, norm, w_gu, cast, *, rows_per_batch):
    f = w_gu.shape[1] // 2
    tn = 512
    return _mm(hc, w_gu, [0, f // tn], f, [], [BF16], _ep_swiglu, name="ffn_gate_up",
               norm=norm, cast=cast, rows_per_batch=rows_per_batch, tm=1024, tn=tn)


def _residual_mm(x, w, h, gate, coef, c1_next, cast, *, rows_per_batch, name):
    tm, tn = 1024, 1024
    tpb = rows_per_batch // tm
    per_batch = lambda i, j, k: (i // tpb, 0, j)
    extras = [(h, (tm, tn), lambda i, j, k: (i, j)), (gate, (1, 1, tn), per_batch)]
    outs = [F32]
    if c1_next is not None:
        extras.append((c1_next, (1, 1, tn), per_batch))
        outs += [BF16, "ssq"]
    return _mm(x, w, [0], w.shape[1], extras, outs, functools.partial(_ep_residual, coef=coef),
               name=name, cast=cast, tm=tm, tn=tn, tk=2048)


def kernel(x, c, ada_w, ada_b, norm_gains, ffn1_w_gu, ffn1_w_down, ffn2_w_gu, ffn2_w_down,
           attn_w_qkv, attn_w_o, attn_q_gain, attn_k_gain, attn_sinks,
           hgrn_w_qfig, hgrn_w_o, hgrn_g_gain, hgrn_lower_bounds):
    batch, seq, d = x.shape
    depth = ada_w.shape[0]
    m = batch * seq
    n_mixers = 2

    c_pad = jnp.zeros((8, d), F32).at[:batch].set(c)
    mod = _ada(c_pad, ada_w, ada_b)
    mods = mod[:, :batch].reshape(depth, batch, N_MOD, 1, d)
    shift = lambda i, n: mods[i, :, 3 * n]
    c1 = lambda i, n: norm_gains[i, n] * (1.0 + mods[i, :, 3 * n + 1])
    gate = lambda i, n: mods[i, :, 3 * n + 2]

    w_gu = _cast_bf16(ffn1_w_gu, 0)
    h = x.reshape(m, d)
    hc, ssq = _row_stats(h, c1(0, 0), rows_per_batch=seq)
    for i in range(depth):
        jm = i // n_mixers
        is_attn = i % n_mixers == 0
        act, w_down = _gate_up(hc, (ssq, shift(i, 0)), w_gu, (ffn1_w_down, i), rows_per_batch=seq)
        h, hc, ssq, w_mix = _residual_mm(act, w_down, h, gate(i, 0), 0.5, c1(i, 1),
                                         (attn_w_qkv, jm) if is_attn else (hgrn_w_qfig, jm),
                                         rows_per_batch=seq, name="ffn_down")
        norm = (ssq, shift(i, 1))
        if is_attn:
            n_heads = attn_sinks.shape[1]
            n_qkv = w_mix.shape[1]
            n_kv = (n_qkv // HEAD_DIM - n_heads) // 2
            tn = n_kv * HEAD_DIM
            q_tiles = n_heads // n_kv
            ones = jnp.ones((1, HEAD_DIM), F32)
            gains = jnp.stack([attn_q_gain[jm].reshape(1, HEAD_DIM) * HEAD_DIM ** -0.5] * q_tiles
                              + [attn_k_gain[jm].reshape(1, HEAD_DIM), ones])
            qkv, w_o = _mm(hc, w_mix, [0], n_qkv,
                           [(gains, (1, 1, HEAD_DIM), lambda i_, j_, k_: (j_, 0, 0))], [BF16],
                           functools.partial(_ep_qkv, n_norm_tiles=q_tiles + 1), name="attn_qkv",
                           norm=norm, cast=(attn_w_o, jm), rows_per_batch=seq, tm=1024, tn=tn)
            heads = jnp.arange(1, n_heads + 1, dtype=F32)
            slopes = jnp.exp2(-8.0 * heads / n_heads)
            y = _attention(qkv, slopes, attn_sinks[jm].astype(F32), batch=batch, seq=seq,
                           n_heads=n_heads, n_kv=n_kv)
            mixer_name = "attn_out"
        else:
            fd = (w_mix.shape[1] - 2 * d) // 2
            tn = 1024
            proj = functools.partial(_mm, hc, w_mix, norm=norm, rows_per_batch=seq, tn=tn)
            qh, w_o = proj([0], fd, [], [BF16], _ep_silu, name="hgrn_q", cast=(hgrn_w_o, jm))
            lbp = hgrn_lower_bounds.astype(F32)
            log_f, kh = proj([fd // tn], fd, [(lbp, (depth, tn), lambda i_, j_, k_: (0, j_))],
                             [F32, BF16], functools.partial(_ep_forget, layer=i), name="hgrn_f")
            vh, = proj([2 * fd // tn], d, [], [BF16], _ep_cast, name="hgrn_i")
            gh, = proj([(2 * fd + d) // tn], d, [], [BF16], _ep_silu, name="hgrn_g")
            y = _hgrn_recurrence(qh, kh, vh, log_f, gh, hgrn_g_gain[jm], batch=batch, seq=seq)
            mixer_name = "hgrn_out"
        h, hc, ssq, w_gu = _residual_mm(y, w_o, h, gate(i, 1), 1.0, c1(i, 2), (ffn2_w_gu, i),
                                        rows_per_batch=seq, name=mixer_name)
        act, w_down = _gate_up(hc, (ssq, shift(i, 2)), w_gu, (ffn2_w_down, i), rows_per_batch=seq)
        if i + 1 < depth:
            h, hc, ssq, w_gu = _residual_mm(act, w_down, h, gate(i, 2), 0.5, c1(i + 1, 0),
                                            (ffn1_w_gu, i + 1), rows_per_batch=seq, name="ffn_down")
        else:
            h, = _residual_mm(act, w_down, h, gate(i, 2), 0.5, None, None, rows_per_batch=seq,
                              name="ffn_down")
    return h.reshape(batch, seq, d)
```

```python
import functools

import numpy as np
import jax
import jax.numpy as jnp
from jax import lax
from jax.experimental import pallas as pl
from jax.experimental.pallas import tpu as pltpu

F32 = jnp.float32
BF16 = jnp.bfloat16
EPS = 1e-6
LOG2_E = 1.4426950408889634

WINDOW = 128
HEAD_DIM = 128
N_MOD = 9
HGRN_CHUNK = 128

V7X_VMEM_BYTES = 64 * 1024 * 1024
VMEM_HEADROOM_BYTES = 6 * 1024 * 1024
LANES = 128


def _vmem_limit(block_bytes):
    want = 2 * block_bytes + VMEM_HEADROOM_BYTES
    return int(min(max(want, 32 * 1024 * 1024), V7X_VMEM_BYTES - 2 * 1024 * 1024))


def _nbytes(shape, dtype):
    return int(np.prod(shape)) * jnp.dtype(dtype).itemsize


def _silu(x):
    return x * jax.nn.sigmoid(x)


def _dot(a, b):
    return jnp.dot(a, b, preferred_element_type=F32)


def _dot_nt(a, b):
    return lax.dot_general(a, b, (((1,), (1,)), ((), ())), preferred_element_type=F32)


def _dot_tn(a, b):
    return lax.dot_general(a, b, (((0,), (0,)), ((), ())), preferred_element_type=F32)


def _ada_kernel(c_ref, w_ref, b_ref, o_ref):
    cond = _silu(c_ref[...]).astype(BF16)
    o_ref[0] = _dot(cond, w_ref[0].astype(BF16)) + b_ref[0]


def _ada(c_pad, ada_w, ada_b, *, tn=512):
    depth, d, nd = ada_w.shape
    rows = c_pad.shape[0]
    blk = _nbytes((d, tn), F32) + _nbytes((rows, tn), F32) + _nbytes((rows, d), F32)
    return pl.pallas_call(
        _ada_kernel,
        grid=(depth, nd // tn),
        in_specs=[
            pl.BlockSpec((rows, d), lambda l, j: (0, 0)),
            pl.BlockSpec((1, d, tn), lambda l, j: (l, 0, j)),
            pl.BlockSpec((1, 1, tn), lambda l, j: (l, 0, j)),
        ],
        out_specs=pl.BlockSpec((1, rows, tn), lambda l, j: (l, 0, j)),
        out_shape=jax.ShapeDtypeStruct((depth, rows, nd), F32),
        compiler_params=pltpu.CompilerParams(
            dimension_semantics=("arbitrary", "arbitrary"),
            vmem_limit_bytes=_vmem_limit(blk)),
        name="ada_mod",
    )(c_pad, ada_w, ada_b.reshape(depth, 1, nd))


def _norm_mod_kernel(h_ref, gain_ref, shift_ref, scale_ref, o_ref):
    h = h_ref[...]
    y = h * lax.rsqrt(jnp.mean(h * h, axis=-1, keepdims=True) + EPS) * gain_ref[...]
    o_ref[...] = (y * (1.0 + scale_ref[0]) + shift_ref[0]).astype(o_ref.dtype)


def _norm_mod(h, gain, shift, scale, *, rows_per_batch, tm=512):
    m, d = h.shape
    tpb = rows_per_batch // tm
    blk = _nbytes((tm, d), F32) + _nbytes((tm, d), BF16)
    return pl.pallas_call(
        _norm_mod_kernel,
        grid=(m // tm,),
        in_specs=[
            pl.BlockSpec((tm, d), lambda i: (i, 0)),
            pl.BlockSpec((1, d), lambda i: (0, 0)),
            pl.BlockSpec((1, 1, d), lambda i: (i // tpb, 0, 0)),
            pl.BlockSpec((1, 1, d), lambda i: (i // tpb, 0, 0)),
        ],
        out_specs=pl.BlockSpec((tm, d), lambda i: (i, 0)),
        out_shape=jax.ShapeDtypeStruct((m, d), BF16),
        compiler_params=pltpu.CompilerParams(
            dimension_semantics=("arbitrary",),
            vmem_limit_bytes=_vmem_limit(blk)),
        name="norm_mod",
    )(h, gain.reshape(1, d), shift, scale)


def _mm_kernel(*refs, n_w, n_e, n_o, nk, n_cast, epilogue):
    x_ref = refs[0]
    w_refs = refs[1:1 + n_w]
    e_refs = refs[1 + n_w:1 + n_w + n_e]
    n_in = 1 + n_w + n_e + n_cast
    o_refs = refs[n_in:n_in + n_o]
    acc_refs = refs[n_in + n_o + n_cast:]
    j = pl.program_id(1)

    for src_ref, dst_ref in zip(refs[n_in - n_cast:n_in], refs[n_in + n_o:n_in + n_o + n_cast]):
        dst_ref[...] = src_ref[...].astype(dst_ref.dtype)

    def dots():
        x = x_ref[...]
        return [_dot(x, w[...]) for w in w_refs]

    if nk == 1:
        epilogue(dots(), j, e_refs, o_refs)
        return

    k = pl.program_id(2)

    @pl.when(k == 0)
    def _():
        for a, d in zip(acc_refs, dots()):
            a[...] = d

    if nk > 2:
        @pl.when((k > 0) & (k < nk - 1))
        def _():
            for a, d in zip(acc_refs, dots()):
                a[...] += d

    @pl.when(k == nk - 1)
    def _():
        epilogue([a[...] + d for a, d in zip(acc_refs, dots())], j, e_refs, o_refs)


def _cast_chunk_rows(n_rows, n_steps):
    rows = 16
    while n_rows % rows or n_rows // rows > n_steps:
        rows += 16
    return rows


def _mm(x, w, w_col_blocks, n_cols, extras, outs, epilogue, *, name, casts=(),
        tm=1024, tn=1024, tk=4096):
    m, kdim = x.shape
    tk = min(tk, kdim)
    nk = kdim // tk
    grid = (m // tm, n_cols // tn, nk)
    in_specs = [pl.BlockSpec((tm, tk), lambda i, j, k: (i, k))]
    blk = _nbytes((tm, tk), x.dtype)
    for off in w_col_blocks:
        in_specs.append(pl.BlockSpec((tk, tn), lambda i, j, k, off=off: (k, j + off)))
        blk += _nbytes((tk, tn), w.dtype)
    for arr, bshape, imap in extras:
        in_specs.append(pl.BlockSpec(bshape, imap))
        blk += _nbytes(bshape, arr.dtype)
    out_specs = [pl.BlockSpec((tm, tn), lambda i, j, k: (i, j)) for _ in outs]
    out_shape = [jax.ShapeDtypeStruct((m, n_cols), dt) for dt in outs]
    blk += sum(_nbytes((tm, tn), dt) for dt in outs)
    operands = [x, *([w] * len(w_col_blocks)), *[e[0] for e in extras]]
    for src, layer in casts:
        _, n_rows, n_src_cols = src.shape
        rows = _cast_chunk_rows(n_rows, grid[0] * grid[1] * grid[2])
        chunk = lambda i, j, k, last=n_rows // rows - 1: jnp.minimum(
            (i * grid[1] + j) * grid[2] + k, last)
        in_specs.append(pl.BlockSpec((None, rows, n_src_cols),
                                     lambda i, j, k, layer=layer, chunk=chunk: (layer, chunk(i, j, k), 0)))
        out_specs.append(pl.BlockSpec((rows, n_src_cols), lambda i, j, k, chunk=chunk: (chunk(i, j, k), 0)))
        out_shape.append(jax.ShapeDtypeStruct((n_rows, n_src_cols), BF16))
        operands.append(src)
        blk += _nbytes((rows, n_src_cols), F32) + _nbytes((rows, n_src_cols), BF16)
    n_w = len(w_col_blocks)
    scratch = [pltpu.VMEM((tm, tn), F32) for _ in range(n_w)] if nk > 1 else []
    scratch_bytes = (len(scratch) + 3 * n_w) * _nbytes((tm, tn), F32)
    kern = functools.partial(_mm_kernel, n_w=n_w, n_e=len(extras), n_o=len(outs), nk=nk,
                             n_cast=len(casts), epilogue=epilogue)
    return pl.pallas_call(
        kern,
        grid=grid,
        in_specs=in_specs,
        out_specs=out_specs,
        out_shape=out_shape,
        scratch_shapes=scratch,
        compiler_params=pltpu.CompilerParams(
            dimension_semantics=("arbitrary", "arbitrary", "arbitrary"),
            vmem_limit_bytes=_vmem_limit(blk + scratch_bytes // 2)),
        name=name,
    )(*operands)


def _cast_kernel(src_ref, o_ref):
    o_ref[...] = src_ref[...].astype(o_ref.dtype)


def _cast_bf16(src, layer, *, rows=128):
    _, n_rows, n_cols = src.shape
    blk = _nbytes((rows, n_cols), F32) + _nbytes((rows, n_cols), BF16)
    return pl.pallas_call(
        _cast_kernel,
        grid=(n_rows // rows,),
        in_specs=[pl.BlockSpec((None, rows, n_cols), lambda r: (layer, r, 0))],
        out_specs=pl.BlockSpec((rows, n_cols), lambda r: (r, 0)),
        out_shape=jax.ShapeDtypeStruct((n_rows, n_cols), BF16),
        compiler_params=pltpu.CompilerParams(
            dimension_semantics=("arbitrary",),
            vmem_limit_bytes=_vmem_limit(blk)),
        name="cast_weights",
    )(src)


def _ep_swiglu(accs, j, e_refs, o_refs):
    a, b = accs
    o_refs[0][...] = (_silu(a) * b).astype(o_refs[0].dtype)


def _ep_residual(accs, j, e_refs, o_refs, *, coef):
    h_ref, gate_ref = e_refs
    o_refs[0][...] = h_ref[...] + (coef * gate_ref[0]) * accs[0]


def _ep_qkv(accs, j, e_refs, o_refs, *, n_norm_tiles):
    y = accs[0]
    gain_ref, = e_refs
    o_ref = o_refs[0]
    tn = y.shape[1]

    @pl.when(j < n_norm_tiles)
    def _():
        g = gain_ref[0]
        for hd in range(tn // HEAD_DIM):
            yh = y[:, hd * HEAD_DIM:(hd + 1) * HEAD_DIM]
            inv = lax.rsqrt(jnp.mean(yh * yh, axis=-1, keepdims=True) + EPS)
            o_ref[:, hd * HEAD_DIM:(hd + 1) * HEAD_DIM] = (yh * inv * g).astype(o_ref.dtype)

    @pl.when(j >= n_norm_tiles)
    def _():
        o_ref[...] = y.astype(o_ref.dtype)


def _ep_silu(accs, j, e_refs, o_refs):
    o_refs[0][...] = _silu(accs[0]).astype(o_refs[0].dtype)


def _ep_cast(accs, j, e_refs, o_refs):
    o_refs[0][...] = accs[0].astype(o_refs[0].dtype)


def _ep_forget(accs, j, e_refs, o_refs, *, layer):
    f = accs[0]
    lbp = e_refs[0][...]
    depth = lbp.shape[0]
    mx = lbp[0:1]
    for l in range(1, depth):
        mx = jnp.maximum(mx, lbp[l:l + 1])
    es = [jnp.exp(lbp[l:l + 1] - mx) for l in range(depth)]
    tot = es[0]
    for l in range(1, depth):
        tot = tot + es[l]
    s = [e / tot for e in es]
    cum = s[0]
    for l in range(1, layer + 1):
        cum = cum + s[l]
    lb = cum - s[0]
    a2 = jnp.log(lb) * LOG2_E
    c2 = jnp.log1p(-lb) * LOG2_E
    f2 = f * LOG2_E
    t = jnp.exp2(-jnp.abs(f2))
    one_t = 1.0 + t
    b2 = (c2 + jnp.minimum(f2, 0.0)) - jnp.log(one_t) * LOG2_E
    e = jnp.exp2(-jnp.abs(a2 - b2))
    o_refs[0][...] = jnp.maximum(a2, b2) + jnp.log(1.0 + e) * LOG2_E
    o_refs[1][...] = ((1.0 - lb) * jnp.where(f >= 0.0, t, 1.0) / one_t).astype(o_refs[1].dtype)


def _split3(x):
    hi = x.astype(BF16)
    r = x - hi.astype(F32)
    mid = r.astype(BF16)
    lo = (r - mid.astype(F32)).astype(BF16)
    return hi, mid, lo


def _attn_tables(sinks, *, n_kv, group, blk):
    n_heads = n_kv * group
    q_feat = jnp.zeros((n_heads, blk, LANES), BF16)
    for i, s_i in enumerate(_split3(sinks)):
        q_feat = q_feat.at[:, :, i].set(jnp.broadcast_to(s_i[:, None], (n_heads, blk)))
    q_feat = q_feat.reshape(n_kv, group * blk, LANES)
    k_feat = np.zeros((2 * blk, LANES), np.float32)
    k_feat[0, 0:3] = 1.0
    qi = np.arange(blk)[:, None]
    kj = np.arange(2 * blk)[None, :]
    dist = (qi + blk - kj).astype(np.float32)
    valid = (dist >= 0) & (dist < WINDOW)
    sink_col = kj == 0
    inf = np.float32(np.inf)
    dist_all = np.where(sink_col, np.float32(0), np.where(valid, dist, inf))
    dist_first = np.where(sink_col, np.float32(0), np.where(valid & (kj >= blk), dist, inf))
    return q_feat, jnp.asarray(k_feat, BF16), jnp.asarray(dist_all), jnp.asarray(dist_first)


def _attn_kernel(slope_ref, qf_ref, kf_ref, dist_ref, dist_first_ref, q_ref, kc_ref, kp_ref, vc_ref,
                 vp_ref, o_ref, d_scr, *, n_kv, group):
    blk = q_ref.shape[0]
    d_scr[...] = jnp.where(pl.program_id(1) > 0, dist_ref[...], dist_first_ref[...])
    keep = (lax.broadcasted_iota(jnp.int32, (16, HEAD_DIM), 0) != 0).astype(BF16)
    ones = jnp.ones((2 * blk, HEAD_DIM), BF16)
    k_feat = kf_ref[...]

    def kv_head(hk, carry):
        ko = pl.ds(pl.multiple_of(hk * HEAD_DIM, HEAD_DIM), HEAD_DIM)
        kp = kp_ref[:, ko]
        vp = vp_ref[:, ko]
        vp = jnp.concatenate([vp[:16] * keep, vp[16:]], axis=0)
        kp = jnp.concatenate([kp[:16] * keep, kp[16:]], axis=0)
        k2 = jnp.concatenate([jnp.concatenate([kp, kc_ref[:, ko]], axis=0), k_feat], axis=1)
        v2 = jnp.concatenate([jnp.concatenate([vp, vc_ref[:, ko]], axis=0), ones], axis=1)
        cols = [pl.ds(pl.multiple_of((hk * group + g) * HEAD_DIM, HEAD_DIM), HEAD_DIM)
                for g in range(group)]
        q4 = jnp.concatenate([q_ref[:, cg] for cg in cols], axis=0)
        q4 = jnp.concatenate([q4, qf_ref[hk]], axis=1)
        s = _dot_nt(q4, k2)
        dist = d_scr[...]
        logits = jnp.concatenate(
            [s[g * blk:(g + 1) * blk] - slope_ref[hk * group + g] * dist for g in range(group)],
            axis=0)
        m = jnp.max(logits, axis=-1, keepdims=True)
        p = jnp.exp(logits - m).astype(BF16)
        o = _dot(p, v2)
        out = o[:, :HEAD_DIM] / o[:, HEAD_DIM:]
        for g, cg in enumerate(cols):
            o_ref[:, cg] = out[g * blk:(g + 1) * blk].astype(o_ref.dtype)
        return carry

    lax.fori_loop(0, n_kv, kv_head, 0, unroll=True)


def _attention(qkv, slopes, sinks, *, batch, seq, n_heads, n_kv):
    m = qkv.shape[0]
    blk = WINDOW
    group = n_heads // n_kv
    nb = seq // blk
    dq = n_heads * HEAD_DIM
    dkv = n_kv * HEAD_DIM
    k_col = dq // dkv
    v_col = k_col + 1
    q_feat, k_feat, dist, dist_first = _attn_tables(sinks, n_kv=n_kv, group=group, blk=blk)
    cur = lambda b, n: b * nb + n
    prev = lambda b, n: b * nb + jnp.maximum(n - 1, 0)
    const2 = lambda b, n: (0, 0)
    blk_bytes = (2 * _nbytes((blk, dq), BF16) + 4 * _nbytes((blk, dkv), BF16)
                 + _nbytes(q_feat.shape, BF16) + 3 * _nbytes(dist.shape, F32))
    return pl.pallas_call(
        functools.partial(_attn_kernel, n_kv=n_kv, group=group),
        grid=(batch, nb),
        in_specs=[
            pl.BlockSpec(memory_space=pltpu.SMEM),
            pl.BlockSpec(q_feat.shape, lambda b, n: (0, 0, 0)),
            pl.BlockSpec(k_feat.shape, const2),
            pl.BlockSpec(dist.shape, const2),
            pl.BlockSpec(dist.shape, const2),
            pl.BlockSpec((blk, dq), lambda b, n: (cur(b, n), 0)),
            pl.BlockSpec((blk, dkv), lambda b, n: (cur(b, n), k_col)),
            pl.BlockSpec((blk, dkv), lambda b, n: (prev(b, n), k_col)),
            pl.BlockSpec((blk, dkv), lambda b, n: (cur(b, n), v_col)),
            pl.BlockSpec((blk, dkv), lambda b, n: (prev(b, n), v_col)),
        ],
        out_specs=pl.BlockSpec((blk, dq), lambda b, n: (cur(b, n), 0)),
        out_shape=jax.ShapeDtypeStruct((m, dq), BF16),
        scratch_shapes=[pltpu.VMEM(dist.shape, F32)],
        compiler_params=pltpu.CompilerParams(
            dimension_semantics=("arbitrary", "arbitrary"),
            vmem_limit_bytes=_vmem_limit(blk_bytes)),
        name="swa_attention",
    )(slopes, q_feat, k_feat, dist, dist_first, qkv, qkv, qkv, qkv, qkv)


def _hgrn_tables(c):
    halves = [c >> (lv + 1) for lv in range(int(np.log2(c)))]
    r = np.arange(c)[:, None]
    j = np.arange(c)[None, :]

    def partial_sums(half):
        ref = (r // (2 * half)) * (2 * half) + half - 1
        return np.where(r > ref, (j > ref) & (j <= r), (j > r) & (j <= ref))

    masks = [((r // (2 * h)) == (j // (2 * h))) & ((r % (2 * h)) >= h) & ((j % (2 * h)) < h)
             for h in halves]
    masks.append(r == j)
    w = np.concatenate([j <= r, partial_sums(4), partial_sums(2)], axis=0).astype(np.float32)
    w3 = np.concatenate([w, w, w], axis=1)
    return jnp.asarray(w3, BF16), jnp.asarray(np.stack(masks).astype(np.float32), BF16), halves


def _hgrn_kernel(w3_ref, mask_ref, gain_ref, q_ref, k_ref, v_ref, lf_ref, gate_ref, o_ref,
                 s_ref, d_scr, *, c, halves):
    @pl.when(pl.program_id(2) == 0)
    def _():
        s_ref[...] = jnp.zeros_like(s_ref)

    n_chunks = q_ref.shape[0] // c
    zero_h = jnp.zeros((c, LANES), BF16)
    n_lv = len(halves)

    def block_diag(x0, x1):
        return jnp.concatenate([jnp.concatenate([x0, zero_h], axis=1),
                                jnp.concatenate([zero_h, x1], axis=1)], axis=0)

    def both(x):
        return x[:, :LANES], x[:, LANES:]

    def pair_mask(i):
        mk = mask_ref[i]
        return jnp.concatenate([mk, mk], axis=1)

    def chunk(ci, carry):
        rows = pl.ds(pl.multiple_of(ci * c, c), c)
        q = lambda: q_ref[rows, :]
        k = lambda: k_ref[rows, :]
        lf3 = jnp.concatenate(_split3(lf_ref[rows, :]), axis=0)
        d_scr[...] = _dot(w3_ref[...], lf3)
        b = lambda: d_scr[0:c, :]

        def level_exponent(half):
            pieces = []
            for g0 in range(0, c, 2 * half):
                ref = jnp.broadcast_to(d_scr[g0 + half - 1:g0 + half, :], (half, 2 * LANES))
                pieces += [ref - d_scr[g0:g0 + half, :], d_scr[g0 + half:g0 + 2 * half, :] - ref]
            return jnp.concatenate(pieces, axis=0)

        e1 = jnp.exp2(lf_ref[rows, :]).astype(BF16)
        p = _dot_nt(jnp.concatenate([q(), q() * e1], axis=0), block_diag(*both(k())))
        a = p[0:c].astype(BF16) * pair_mask(n_lv) + p[c:2 * c].astype(BF16) * pair_mask(n_lv - 1)
        for li, half in enumerate(halves[:-1]):
            if half >= 8:
                dn = level_exponent(half)
            else:
                dn = d_scr[c:2 * c, :] if half == 4 else d_scr[2 * c:3 * c, :]
            e = jnp.exp2(dn).astype(BF16)
            a = a + _dot_nt(q() * e, block_diag(*both(k() * e))).astype(BF16) * pair_mask(li)
        intra = _dot(a, block_diag(*both(v_ref[rows, :])))

        s0 = s_ref[0]
        s1 = s_ref[1]
        inter = _dot_nt(q() * jnp.exp2(b()).astype(BF16),
                        block_diag(s0.astype(BF16), s1.astype(BF16)))
        o = inter + intra

        b_last = d_scr[c - 1:c, :]
        u = _dot_tn(v_ref[rows, :], k() * jnp.exp2(b_last - b()).astype(BF16))
        decay = jnp.exp2(b_last)
        s_ref[0] = s0 * decay[:, :LANES] + u[:LANES, :LANES]
        s_ref[1] = s1 * decay[:, LANES:] + u[LANES:, LANES:]

        gate = gate_ref[rows, :].astype(F32)
        gain = gain_ref[...]
        for hh in range(2):
            lanes = slice(hh * LANES, (hh + 1) * LANES)
            oh = o[:, lanes]
            y = oh * lax.rsqrt(jnp.mean(oh * oh, axis=-1, keepdims=True) + EPS) * gain
            o_ref[rows, lanes] = (y * gate[:, lanes]).astype(o_ref.dtype)
        return carry

    lax.fori_loop(0, n_chunks, chunk, 0, unroll=min(n_chunks, 8))


def _hgrn_recurrence(q, k, v, log_f, gate, gain, *, batch, seq, tb=2048):
    m, d = q.shape
    c = HGRN_CHUNK
    w3, masks, halves = _hgrn_tables(c)
    tb = min(tb, seq)
    nt = seq // tb
    width = 2 * HEAD_DIM
    tile = pl.BlockSpec((tb, width), lambda b, h, t: (b * nt + t, h))
    const2 = lambda b, h, t: (0, 0)
    blk = (_nbytes((tb, width), F32) + 5 * _nbytes((tb, width), BF16)
           + _nbytes(w3.shape, BF16) + _nbytes(masks.shape, BF16))
    return pl.pallas_call(
        functools.partial(_hgrn_kernel, c=c, halves=halves),
        grid=(batch, d // width, nt),
        in_specs=[
            pl.BlockSpec(w3.shape, const2),
            pl.BlockSpec(masks.shape, lambda b, h, t: (0, 0, 0)),
            pl.BlockSpec((1, HEAD_DIM), const2),
            tile, tile, tile, tile, tile,
        ],
        out_specs=tile,
        out_shape=jax.ShapeDtypeStruct((m, d), BF16),
        scratch_shapes=[pltpu.VMEM((2, HEAD_DIM, HEAD_DIM), F32), pltpu.VMEM((3 * c, width), F32)],
        compiler_params=pltpu.CompilerParams(
            dimension_semantics=("arbitrary", "arbitrary", "arbitrary"),
            vmem_limit_bytes=_vmem_limit(blk)),
        name="hgrn2_recurrence",
    )(w3, masks, gain.reshape(1, HEAD_DIM), q, k, v, log_f, gate)


def _bcast_mod(mod_l, idx, batch):
    d = mod_l.shape[-1] // N_MOD
    return mod_l[:batch, idx * d:(idx + 1) * d].reshape(batch, 1, d)


def _gate_up(u, w_gu, casts):
    f = w_gu.shape[1] // 2
    tn = 512
    return _mm(u, w_gu, [0, f // tn], f, [], [BF16], _ep_swiglu, name="ffn_gate_up",
               casts=casts, tm=1024, tn=tn)


def _residual_mm(x, w, h, gate, coef, *, rows_per_batch, name):
    tm, tn = 1024, 1024
    tpb = rows_per_batch // tm
    extras = [
        (h, (tm, tn), lambda i, j, k: (i, j)),
        (gate, (1, 1, tn), lambda i, j, k: (i // tpb, 0, j)),
    ]
    out, = _mm(x, w, [0], w.shape[1], extras, [F32], functools.partial(_ep_residual, coef=coef),
               name=name, tm=tm, tn=tn, tk=2048)
    return out


def kernel(x, c, ada_w, ada_b, norm_gains, ffn1_w_gu, ffn1_w_down, ffn2_w_gu, ffn2_w_down,
           attn_w_qkv, attn_w_o, attn_q_gain, attn_k_gain, attn_sinks,
           hgrn_w_qfig, hgrn_w_o, hgrn_g_gain, hgrn_lower_bounds):
    batch, seq, d = x.shape
    depth = ada_w.shape[0]
    m = batch * seq
    n_mixers = 2

    c_pad = jnp.zeros((8, d), F32).at[:batch].set(c)
    mod = _ada(c_pad, ada_w, ada_b)

    w_gu = _cast_bf16(ffn1_w_gu, 0)
    h = x.reshape(m, d)
    for i in range(depth):
        md = lambda idx: _bcast_mod(mod[i], idx, batch)
        jm = i // n_mixers
        is_attn = i % n_mixers == 0
        u = _norm_mod(h, norm_gains[i, 0], md(0), md(1), rows_per_batch=seq)
        mixer_w = (attn_w_qkv, attn_w_o) if is_attn else (hgrn_w_qfig, hgrn_w_o)
        act, w_down, w_mix, w_o, w_gu = _gate_up(
            u, w_gu, [(ffn1_w_down, i), (mixer_w[0], jm), (mixer_w[1], jm), (ffn2_w_gu, i)])
        h = _residual_mm(act, w_down, h, md(2), 0.5, rows_per_batch=seq, name="ffn_down")
        u = _norm_mod(h, norm_gains[i, 1], md(3), md(4), rows_per_batch=seq)
        if is_attn:
            n_heads = attn_sinks.shape[1]
            n_qkv = w_mix.shape[1]
            n_kv = (n_qkv // HEAD_DIM - n_heads) // 2
            tn = n_kv * HEAD_DIM
            q_tiles = n_heads // n_kv
            ones = jnp.ones((1, HEAD_DIM), F32)
            gains = jnp.stack([attn_q_gain[jm].reshape(1, HEAD_DIM) * HEAD_DIM ** -0.5] * q_tiles
                              + [attn_k_gain[jm].reshape(1, HEAD_DIM), ones])
            qkv, = _mm(u, w_mix, [0], n_qkv,
                       [(gains, (1, 1, HEAD_DIM), lambda i_, j_, k_: (j_, 0, 0))], [BF16],
                       functools.partial(_ep_qkv, n_norm_tiles=q_tiles + 1),
                       name="attn_qkv", tm=1024, tn=tn)
            heads = jnp.arange(1, n_heads + 1, dtype=F32)
            slopes = jnp.exp2(-8.0 * heads / n_heads)
            y = _attention(qkv, slopes, attn_sinks[jm].astype(F32), batch=batch, seq=seq,
                           n_heads=n_heads, n_kv=n_kv)
            mixer_name = "attn_out"
        else:
            fd = (w_mix.shape[1] - 2 * d) // 2
            tn = 1024
            qh, = _mm(u, w_mix, [0], fd, [], [BF16], _ep_silu, name="hgrn_q", tn=tn)
            lbp = hgrn_lower_bounds.astype(F32)
            log_f, kh = _mm(u, w_mix, [fd // tn], fd,
                            [(lbp, (depth, tn), lambda i_, j_, k_: (0, j_))], [F32, BF16],
                            functools.partial(_ep_forget, layer=i), name="hgrn_f", tn=tn)
            vh, = _mm(u, w_mix, [2 * fd // tn], d, [], [BF16], _ep_cast, name="hgrn_i", tn=tn)
            gh, = _mm(u, w_mix, [(2 * fd + d) // tn], d, [], [BF16], _ep_silu, name="hgrn_g", tn=tn)
            y = _hgrn_recurrence(qh, kh, vh, log_f, gh, hgrn_g_gain[jm], batch=batch, seq=seq)
            mixer_name = "hgrn_out"
        h = _residual_mm(y, w_o, h, md(5), 1.0, rows_per_batch=seq, name=mixer_name)
        u = _norm_mod(h, norm_gains[i, 2], md(6), md(7), rows_per_batch=seq)
        later = [(ffn2_w_down, i)] + ([(ffn1_w_gu, i + 1)] if i + 1 < depth else [])
        act, w_down, *next_gu = _gate_up(u, w_gu, later)
        h = _residual_mm(act, w_down, h, md(8), 0.5, rows_per_batch=seq, name="ffn_down")
        if next_gu:
            w_gu, = next_gu
    return h.reshape(batch, seq, d)
```

```python
import functools

import numpy as np
import jax
import jax.numpy as jnp
from jax import lax
from jax.experimental import pallas as pl
from jax.experimental.pallas import tpu as pltpu

F32 = jnp.float32
BF16 = jnp.bfloat16
EPS = 1e-6
LOG2_E = 1.4426950408889634

WINDOW = 128
HEAD_DIM = 128
N_MOD = 9
HGRN_CHUNK = 128

V7X_VMEM_BYTES = 64 * 1024 * 1024
VMEM_HEADROOM_BYTES = 6 * 1024 * 1024
LANES = 128


def _vmem_limit(block_bytes):
    want = 2 * block_bytes + VMEM_HEADROOM_BYTES
    return int(min(max(want, 32 * 1024 * 1024), V7X_VMEM_BYTES - 2 * 1024 * 1024))


def _nbytes(shape, dtype):
    return int(np.prod(shape)) * jnp.dtype(dtype).itemsize


def _silu(x):
    half = 0.5 * x
    return half + half * jnp.tanh(half)


def _dot(a, b):
    return jnp.dot(a, b, preferred_element_type=F32)


def _dot_nt(a, b):
    return lax.dot_general(a, b, (((1,), (1,)), ((), ())), preferred_element_type=F32)


def _dot_tn(a, b):
    return lax.dot_general(a, b, (((0,), (0,)), ((), ())), preferred_element_type=F32)


def _ada_kernel(c_ref, w_ref, b_ref, o_ref):
    cond = _silu(c_ref[...]).astype(BF16)
    o_ref[0] = _dot(cond, w_ref[0].astype(BF16)) + b_ref[0]


def _ada(c_pad, ada_w, ada_b, *, tn=512):
    depth, d, nd = ada_w.shape
    rows = c_pad.shape[0]
    blk = _nbytes((d, tn), F32) + _nbytes((rows, tn), F32) + _nbytes((rows, d), F32)
    return pl.pallas_call(
        _ada_kernel,
        grid=(depth, nd // tn),
        in_specs=[
            pl.BlockSpec((rows, d), lambda l, j: (0, 0)),
            pl.BlockSpec((1, d, tn), lambda l, j: (l, 0, j)),
            pl.BlockSpec((1, 1, tn), lambda l, j: (l, 0, j)),
        ],
        out_specs=pl.BlockSpec((1, rows, tn), lambda l, j: (l, 0, j)),
        out_shape=jax.ShapeDtypeStruct((depth, rows, nd), F32),
        compiler_params=pltpu.CompilerParams(
            dimension_semantics=("arbitrary", "arbitrary"),
            vmem_limit_bytes=_vmem_limit(blk)),
        name="ada_mod",
    )(c_pad, ada_w, ada_b.reshape(depth, 1, nd))


def _norm_mod_kernel(h_ref, gain_ref, shift_ref, scale_ref, o_ref):
    h = h_ref[...]
    y = h * lax.rsqrt(jnp.mean(h * h, axis=-1, keepdims=True) + EPS) * gain_ref[...]
    o_ref[...] = (y * (1.0 + scale_ref[0]) + shift_ref[0]).astype(o_ref.dtype)


def _norm_mod(h, gain, shift, scale, *, rows_per_batch, tm=512):
    m, d = h.shape
    tpb = rows_per_batch // tm
    blk = _nbytes((tm, d), F32) + _nbytes((tm, d), BF16)
    return pl.pallas_call(
        _norm_mod_kernel,
        grid=(m // tm,),
        in_specs=[
            pl.BlockSpec((tm, d), lambda i: (i, 0)),
            pl.BlockSpec((1, d), lambda i: (0, 0)),
            pl.BlockSpec((1, 1, d), lambda i: (i // tpb, 0, 0)),
            pl.BlockSpec((1, 1, d), lambda i: (i // tpb, 0, 0)),
        ],
        out_specs=pl.BlockSpec((tm, d), lambda i: (i, 0)),
        out_shape=jax.ShapeDtypeStruct((m, d), BF16),
        compiler_params=pltpu.CompilerParams(
            dimension_semantics=("arbitrary",),
            vmem_limit_bytes=_vmem_limit(blk)),
        name="norm_mod",
    )(h, gain.reshape(1, d), shift, scale)


def _mm_kernel(*refs, n_w, n_e, n_o, nk, n_cast, epilogue):
    x_ref = refs[0]
    w_refs = refs[1:1 + n_w]
    e_refs = refs[1 + n_w:1 + n_w + n_e]
    n_in = 1 + n_w + n_e + n_cast
    o_refs = refs[n_in:n_in + n_o]
    acc_refs = refs[n_in + n_o + n_cast:]
    j = pl.program_id(1)

    for src_ref, dst_ref in zip(refs[n_in - n_cast:n_in], refs[n_in + n_o:n_in + n_o + n_cast]):
        dst_ref[...] = src_ref[...].astype(dst_ref.dtype)

    def dots():
        x = x_ref[...]
        return [_dot(x, w[...]) for w in w_refs]

    if nk == 1:
        epilogue(dots(), j, e_refs, o_refs)
        return

    k = pl.program_id(2)

    @pl.when(k == 0)
    def _():
        for a, d in zip(acc_refs, dots()):
            a[...] = d

    if nk > 2:
        @pl.when((k > 0) & (k < nk - 1))
        def _():
            for a, d in zip(acc_refs, dots()):
                a[...] += d

    @pl.when(k == nk - 1)
    def _():
        epilogue([a[...] + d for a, d in zip(acc_refs, dots())], j, e_refs, o_refs)


def _cast_chunk_rows(n_rows, n_steps):
    rows = 16
    while n_rows % rows or n_rows // rows > n_steps:
        rows += 16
    return rows


def _mm(x, w, w_col_blocks, n_cols, extras, outs, epilogue, *, name, casts=(),
        tm=1024, tn=1024, tk=4096):
    m, kdim = x.shape
    tk = min(tk, kdim)
    nk = kdim // tk
    grid = (m // tm, n_cols // tn, nk)
    in_specs = [pl.BlockSpec((tm, tk), lambda i, j, k: (i, k))]
    blk = _nbytes((tm, tk), x.dtype)
    for off in w_col_blocks:
        in_specs.append(pl.BlockSpec((tk, tn), lambda i, j, k, off=off: (k, j + off)))
        blk += _nbytes((tk, tn), w.dtype)
    for arr, bshape, imap in extras:
        in_specs.append(pl.BlockSpec(bshape, imap))
        blk += _nbytes(bshape, arr.dtype)
    out_specs = [pl.BlockSpec((tm, tn), lambda i, j, k: (i, j)) for _ in outs]
    out_shape = [jax.ShapeDtypeStruct((m, n_cols), dt) for dt in outs]
    blk += sum(_nbytes((tm, tn), dt) for dt in outs)
    operands = [x, *([w] * len(w_col_blocks)), *[e[0] for e in extras]]
    for src, layer in casts:
        _, n_rows, n_src_cols = src.shape
        rows = _cast_chunk_rows(n_rows, grid[0] * grid[1] * grid[2])
        chunk = lambda i, j, k, last=n_rows // rows - 1: jnp.minimum(
            (i * grid[1] + j) * grid[2] + k, last)
        in_specs.append(pl.BlockSpec((None, rows, n_src_cols),
                                     lambda i, j, k, layer=layer, chunk=chunk: (layer, chunk(i, j, k), 0)))
        out_specs.append(pl.BlockSpec((rows, n_src_cols), lambda i, j, k, chunk=chunk: (chunk(i, j, k), 0)))
        out_shape.append(jax.ShapeDtypeStruct((n_rows, n_src_cols), BF16))
        operands.append(src)
        blk += _nbytes((rows, n_src_cols), F32) + _nbytes((rows, n_src_cols), BF16)
    n_w = len(w_col_blocks)
    scratch = [pltpu.VMEM((tm, tn), F32) for _ in range(n_w)] if nk > 1 else []
    scratch_bytes = (len(scratch) + 3 * n_w) * _nbytes((tm, tn), F32)
    kern = functools.partial(_mm_kernel, n_w=n_w, n_e=len(extras), n_o=len(outs), nk=nk,
                             n_cast=len(casts), epilogue=epilogue)
    return pl.pallas_call(
        kern,
        grid=grid,
        in_specs=in_specs,
        out_specs=out_specs,
        out_shape=out_shape,
        scratch_shapes=scratch,
        compiler_params=pltpu.CompilerParams(
            dimension_semantics=("arbitrary", "arbitrary", "arbitrary"),
            vmem_limit_bytes=_vmem_limit(blk + scratch_bytes // 2)),
        name=name,
    )(*operands)


def _cast_kernel(src_ref, o_ref):
    o_ref[...] = src_ref[...].astype(o_ref.dtype)


def _cast_bf16(src, layer, *, rows=128):
    _, n_rows, n_cols = src.shape
    blk = _nbytes((rows, n_cols), F32) + _nbytes((rows, n_cols), BF16)
    return pl.pallas_call(
        _cast_kernel,
        grid=(n_rows // rows,),
        in_specs=[pl.BlockSpec((None, rows, n_cols), lambda r: (layer, r, 0))],
        out_specs=pl.BlockSpec((rows, n_cols), lambda r: (r, 0)),
        out_shape=jax.ShapeDtypeStruct((n_rows, n_cols), BF16),
        compiler_params=pltpu.CompilerParams(
            dimension_semantics=("arbitrary",),
            vmem_limit_bytes=_vmem_limit(blk)),
        name="cast_weights",
    )(src)


def _ep_swiglu(accs, j, e_refs, o_refs):
    a, b = accs
    o_refs[0][...] = (_silu(a) * b).astype(o_refs[0].dtype)


def _ep_residual(accs, j, e_refs, o_refs, *, coef):
    h_ref, gate_ref = e_refs
    o_refs[0][...] = h_ref[...] + (coef * gate_ref[0]) * accs[0]


def _ep_qkv(accs, j, e_refs, o_refs, *, n_norm_tiles):
    y = accs[0]
    gain_ref, = e_refs
    o_ref = o_refs[0]
    tn = y.shape[1]

    @pl.when(j < n_norm_tiles)
    def _():
        g = gain_ref[0]
        for hd in range(tn // HEAD_DIM):
            yh = y[:, hd * HEAD_DIM:(hd + 1) * HEAD_DIM]
            inv = lax.rsqrt(jnp.mean(yh * yh, axis=-1, keepdims=True) + EPS)
            o_ref[:, hd * HEAD_DIM:(hd + 1) * HEAD_DIM] = (yh * inv * g).astype(o_ref.dtype)

    @pl.when(j >= n_norm_tiles)
    def _():
        o_ref[...] = y.astype(o_ref.dtype)


def _ep_silu(accs, j, e_refs, o_refs):
    o_refs[0][...] = _silu(accs[0]).astype(o_refs[0].dtype)


def _ep_cast(accs, j, e_refs, o_refs):
    o_refs[0][...] = accs[0].astype(o_refs[0].dtype)


def _ep_forget(accs, j, e_refs, o_refs, *, layer):
    f = accs[0]
    lbp = e_refs[0][...]
    depth = lbp.shape[0]
    mx = lbp[0:1]
    for l in range(1, depth):
        mx = jnp.maximum(mx, lbp[l:l + 1])
    es = [jnp.exp(lbp[l:l + 1] - mx) for l in range(depth)]
    tot = es[0]
    for l in range(1, depth):
        tot = tot + es[l]
    s = [e / tot for e in es]
    cum = s[0]
    for l in range(1, layer + 1):
        cum = cum + s[l]
    lb = cum - s[0]
    a2 = jnp.log(lb) * LOG2_E
    c2 = jnp.log1p(-lb) * LOG2_E
    f2 = f * LOG2_E
    t = jnp.exp2(-jnp.abs(f2))
    one_t = 1.0 + t
    b2 = (c2 + jnp.minimum(f2, 0.0)) - jnp.log(one_t) * LOG2_E
    e = jnp.exp2(-jnp.abs(a2 - b2))
    o_refs[0][...] = jnp.maximum(a2, b2) + jnp.log(1.0 + e) * LOG2_E
    o_refs[1][...] = ((1.0 - lb) * jnp.where(f >= 0.0, t, 1.0) / one_t).astype(o_refs[1].dtype)


def _split3(x):
    hi = x.astype(BF16)
    r = x - hi.astype(F32)
    mid = r.astype(BF16)
    lo = (r - mid.astype(F32)).astype(BF16)
    return hi, mid, lo


def _attn_tables(sinks, *, n_kv, group, blk):
    n_heads = n_kv * group
    q_feat = jnp.zeros((n_heads, blk, LANES), BF16)
    for i, s_i in enumerate(_split3(sinks)):
        q_feat = q_feat.at[:, :, i].set(jnp.broadcast_to(s_i[:, None], (n_heads, blk)))
    q_feat = q_feat.reshape(n_kv, group * blk, LANES)
    k_feat = np.zeros((2 * blk, LANES), np.float32)
    k_feat[0, 0:3] = 1.0
    qi = np.arange(blk)[:, None]
    kj = np.arange(2 * blk)[None, :]
    dist = (qi + blk - kj).astype(np.float32)
    valid = (dist >= 0) & (dist < WINDOW)
    sink_col = kj == 0
    inf = np.float32(np.inf)
    dist_all = np.where(sink_col, np.float32(0), np.where(valid, dist, inf))
    dist_first = np.where(sink_col, np.float32(0), np.where(valid & (kj >= blk), dist, inf))
    return q_feat, jnp.asarray(k_feat, BF16), jnp.asarray(dist_all), jnp.asarray(dist_first)


def _attn_kernel(slope_ref, qf_ref, kf_ref, dist_ref, dist_first_ref, q_ref, kc_ref, kp_ref, vc_ref,
                 vp_ref, o_ref, d_scr, *, n_kv, group):
    blk = q_ref.shape[0]
    d_scr[...] = jnp.where(pl.program_id(1) > 0, dist_ref[...], dist_first_ref[...])
    keep = (lax.broadcasted_iota(jnp.int32, (16, HEAD_DIM), 0) != 0).astype(BF16)
    ones = jnp.ones((2 * blk, HEAD_DIM), BF16)
    k_feat = kf_ref[...]

    def kv_head(hk, carry):
        ko = pl.ds(pl.multiple_of(hk * HEAD_DIM, HEAD_DIM), HEAD_DIM)
        kp = kp_ref[:, ko]
        vp = vp_ref[:, ko]
        vp = jnp.concatenate([vp[:16] * keep, vp[16:]], axis=0)
        kp = jnp.concatenate([kp[:16] * keep, kp[16:]], axis=0)
        k2 = jnp.concatenate([jnp.concatenate([kp, kc_ref[:, ko]], axis=0), k_feat], axis=1)
        v2 = jnp.concatenate([jnp.concatenate([vp, vc_ref[:, ko]], axis=0), ones], axis=1)
        cols = [pl.ds(pl.multiple_of((hk * group + g) * HEAD_DIM, HEAD_DIM), HEAD_DIM)
                for g in range(group)]
        q4 = jnp.concatenate([q_ref[:, cg] for cg in cols], axis=0)
        q4 = jnp.concatenate([q4, qf_ref[hk]], axis=1)
        s = _dot_nt(q4, k2)
        dist = d_scr[...]
        logits = jnp.concatenate(
            [s[g * blk:(g + 1) * blk] - slope_ref[hk * group + g] * dist for g in range(group)],
            axis=0)
        m = jnp.max(logits, axis=-1, keepdims=True)
        p = jnp.exp(logits - m).astype(BF16)
        o = _dot(p, v2)
        out = o[:, :HEAD_DIM] / o[:, HEAD_DIM:]
        for g, cg in enumerate(cols):
            o_ref[:, cg] = out[g * blk:(g + 1) * blk].astype(o_ref.dtype)
        return carry

    lax.fori_loop(0, n_kv, kv_head, 0, unroll=True)


def _attention(qkv, slopes, sinks, *, batch, seq, n_heads, n_kv):
    m = qkv.shape[0]
    blk = WINDOW
    group = n_heads // n_kv
    nb = seq // blk
    dq = n_heads * HEAD_DIM
    dkv = n_kv * HEAD_DIM
    k_col = dq // dkv
    v_col = k_col + 1
    q_feat, k_feat, dist, dist_first = _attn_tables(sinks, n_kv=n_kv, group=group, blk=blk)
    cur = lambda b, n: b * nb + n
    prev = lambda b, n: b * nb + jnp.maximum(n - 1, 0)
    const2 = lambda b, n: (0, 0)
    blk_bytes = (2 * _nbytes((blk, dq), BF16) + 4 * _nbytes((blk, dkv), BF16)
                 + _nbytes(q_feat.shape, BF16) + 3 * _nbytes(dist.shape, F32))
    return pl.pallas_call(
        functools.partial(_attn_kernel, n_kv=n_kv, group=group),
        grid=(batch, nb),
        in_specs=[
            pl.BlockSpec(memory_space=pltpu.SMEM),
            pl.BlockSpec(q_feat.shape, lambda b, n: (0, 0, 0)),
            pl.BlockSpec(k_feat.shape, const2),
            pl.BlockSpec(dist.shape, const2),
            pl.BlockSpec(dist.shape, const2),
            pl.BlockSpec((blk, dq), lambda b, n: (cur(b, n), 0)),
            pl.BlockSpec((blk, dkv), lambda b, n: (cur(b, n), k_col)),
            pl.BlockSpec((blk, dkv), lambda b, n: (prev(b, n), k_col)),
            pl.BlockSpec((blk, dkv), lambda b, n: (cur(b, n), v_col)),
            pl.BlockSpec((blk, dkv), lambda b, n: (prev(b, n), v_col)),
        ],
        out_specs=pl.BlockSpec((blk, dq), lambda b, n: (cur(b, n), 0)),
        out_shape=jax.ShapeDtypeStruct((m, dq), BF16),
        scratch_shapes=[pltpu.VMEM(dist.shape, F32)],
        compiler_params=pltpu.CompilerParams(
            dimension_semantics=("arbitrary", "arbitrary"),
            vmem_limit_bytes=_vmem_limit(blk_bytes)),
        name="swa_attention",
    )(slopes, q_feat, k_feat, dist, dist_first, qkv, qkv, qkv, qkv, qkv)


def _hgrn_tables(c):
    halves = [c >> (lv + 1) for lv in range(int(np.log2(c)))]
    r = np.arange(c)[:, None]
    j = np.arange(c)[None, :]

    def partial_sums(half):
        ref = (r // (2 * half)) * (2 * half) + half - 1
        return np.where(r > ref, (j > ref) & (j <= r), (j > r) & (j <= ref))

    masks = [((r // (2 * h)) == (j // (2 * h))) & ((r % (2 * h)) >= h) & ((j % (2 * h)) < h)
             for h in halves]
    masks.append(r == j)
    w = np.concatenate([j <= r, partial_sums(4), partial_sums(2)], axis=0).astype(np.float32)
    w3 = np.concatenate([w, w, w], axis=1)
    return jnp.asarray(w3, BF16), jnp.asarray(np.stack(masks).astype(np.float32), BF16), halves


def _hgrn_kernel(w3_ref, mask_ref, gain_ref, q_ref, k_ref, v_ref, lf_ref, gate_ref, o_ref,
                 s_ref, d_scr, *, c, halves):
    @pl.when(pl.program_id(2) == 0)
    def _():
        s_ref[...] = jnp.zeros_like(s_ref)

    n_chunks = q_ref.shape[0] // c
    zero_h = jnp.zeros((c, LANES), BF16)
    n_lv = len(halves)

    def block_diag(x0, x1):
        return jnp.concatenate([jnp.concatenate([x0, zero_h], axis=1),
                                jnp.concatenate([zero_h, x1], axis=1)], axis=0)

    def both(x):
        return x[:, :LANES], x[:, LANES:]

    def pair_mask(i):
        mk = mask_ref[i]
        return jnp.concatenate([mk, mk], axis=1)

    def chunk(ci, carry):
        rows = pl.ds(pl.multiple_of(ci * c, c), c)
        q = lambda: q_ref[rows, :]
        k = lambda: k_ref[rows, :]
        lf3 = jnp.concatenate(_split3(lf_ref[rows, :]), axis=0)
        d_scr[...] = _dot(w3_ref[...], lf3)
        b = lambda: d_scr[0:c, :]

        def level_exponent(half):
            pieces = []
            for g0 in range(0, c, 2 * half):
                ref = jnp.broadcast_to(d_scr[g0 + half - 1:g0 + half, :], (half, 2 * LANES))
                pieces += [ref - d_scr[g0:g0 + half, :], d_scr[g0 + half:g0 + 2 * half, :] - ref]
            return jnp.concatenate(pieces, axis=0)

        e1 = jnp.exp2(lf_ref[rows, :]).astype(BF16)
        p = _dot_nt(jnp.concatenate([q(), q() * e1], axis=0), block_diag(*both(k())))
        a = p[0:c].astype(BF16) * pair_mask(n_lv) + p[c:2 * c].astype(BF16) * pair_mask(n_lv - 1)
        for li, half in enumerate(halves[:-1]):
            if half >= 8:
                dn = level_exponent(half)
            else:
                dn = d_scr[c:2 * c, :] if half == 4 else d_scr[2 * c:3 * c, :]
            e = jnp.exp2(dn).astype(BF16)
            a = a + _dot_nt(q() * e, block_diag(*both(k() * e))).astype(BF16) * pair_mask(li)
        intra = _dot(a, block_diag(*both(v_ref[rows, :])))

        s0 = s_ref[0]
        s1 = s_ref[1]
        inter = _dot_nt(q() * jnp.exp2(b()).astype(BF16),
                        block_diag(s0.astype(BF16), s1.astype(BF16)))
        o = inter + intra

        b_last = d_scr[c - 1:c, :]
        u = _dot_tn(v_ref[rows, :], k() * jnp.exp2(b_last - b()).astype(BF16))
        decay = jnp.exp2(b_last)
        s_ref[0] = s0 * decay[:, :LANES] + u[:LANES, :LANES]
        s_ref[1] = s1 * decay[:, LANES:] + u[LANES:, LANES:]

        gate = gate_ref[rows, :].astype(F32)
        gain = gain_ref[...]
        for hh in range(2):
            lanes = slice(hh * LANES, (hh + 1) * LANES)
            oh = o[:, lanes]
            y = oh * lax.rsqrt(jnp.mean(oh * oh, axis=-1, keepdims=True) + EPS) * gain
            o_ref[rows, lanes] = (y * gate[:, lanes]).astype(o_ref.dtype)
        return carry

    lax.fori_loop(0, n_chunks, chunk, 0, unroll=min(n_chunks, 8))


def _hgrn_recurrence(q, k, v, log_f, gate, gain, *, batch, seq, tb=2048):
    m, d = q.shape
    c = HGRN_CHUNK
    w3, masks, halves = _hgrn_tables(c)
    tb = min(tb, seq)
    nt = seq // tb
    width = 2 * HEAD_DIM
    tile = pl.BlockSpec((tb, width), lambda b, h, t: (b * nt + t, h))
    const2 = lambda b, h, t: (0, 0)
    blk = (_nbytes((tb, width), F32) + 5 * _nbytes((tb, width), BF16)
           + _nbytes(w3.shape, BF16) + _nbytes(masks.shape, BF16))
    return pl.pallas_call(
        functools.partial(_hgrn_kernel, c=c, halves=halves),
        grid=(batch, d // width, nt),
        in_specs=[
            pl.BlockSpec(w3.shape, const2),
            pl.BlockSpec(masks.shape, lambda b, h, t: (0, 0, 0)),
            pl.BlockSpec((1, HEAD_DIM), const2),
            tile, tile, tile, tile, tile,
        ],
        out_specs=tile,
        out_shape=jax.ShapeDtypeStruct((m, d), BF16),
        scratch_shapes=[pltpu.VMEM((2, HEAD_DIM, HEAD_DIM), F32), pltpu.VMEM((3 * c, width), F32)],
        compiler_params=pltpu.CompilerParams(
            dimension_semantics=("arbitrary", "arbitrary", "arbitrary"),
            vmem_limit_bytes=_vmem_limit(blk)),
        name="hgrn2_recurrence",
    )(w3, masks, gain.reshape(1, HEAD_DIM), q, k, v, log_f, gate)


def _bcast_mod(mod_l, idx, batch):
    d = mod_l.shape[-1] // N_MOD
    return mod_l[:batch, idx * d:(idx + 1) * d].reshape(batch, 1, d)


def _gate_up(u, w_gu, casts):
    f = w_gu.shape[1] // 2
    tn = 512
    return _mm(u, w_gu, [0, f // tn], f, [], [BF16], _ep_swiglu, name="ffn_gate_up",
               casts=casts, tm=1024, tn=tn)


def _residual_mm(x, w, h, gate, coef, *, rows_per_batch, name):
    tm, tn = 1024, 1024
    tpb = rows_per_batch // tm
    extras = [
        (h, (tm, tn), lambda i, j, k: (i, j)),
        (gate, (1, 1, tn), lambda i, j, k: (i // tpb, 0, j)),
    ]
    tk = x.shape[1] if x.shape[1] <= 4096 else 2048
    out, = _mm(x, w, [0], w.shape[1], extras, [F32], functools.partial(_ep_residual, coef=coef),
               name=name, tm=tm, tn=tn, tk=tk)
    return out


def kernel(x, c, ada_w, ada_b, norm_gains, ffn1_w_gu, ffn1_w_down, ffn2_w_gu, ffn2_w_down,
           attn_w_qkv, attn_w_o, attn_q_gain, attn_k_gain, attn_sinks,
           hgrn_w_qfig, hgrn_w_o, hgrn_g_gain, hgrn_lower_bounds):
    batch, seq, d = x.shape
    depth = ada_w.shape[0]
    m = batch * seq
    n_mixers = 2

    c_pad = jnp.zeros((8, d), F32).at[:batch].set(c)
    mod = _ada(c_pad, ada_w, ada_b)

    w_gu = _cast_bf16(ffn1_w_gu, 0)
    h = x.reshape(m, d)
    for i in range(depth):
        md = lambda idx: _bcast_mod(mod[i], idx, batch)
        jm = i // n_mixers
        is_attn = i % n_mixers == 0
        u = _norm_mod(h, norm_gains[i, 0], md(0), md(1), rows_per_batch=seq)
        mixer_w = (attn_w_qkv, attn_w_o) if is_attn else (hgrn_w_qfig, hgrn_w_o)
        act, w_down, w_mix, w_o, w_gu = _gate_up(
            u, w_gu, [(ffn1_w_down, i), (mixer_w[0], jm), (mixer_w[1], jm), (ffn2_w_gu, i)])
        h = _residual_mm(act, w_down, h, md(2), 0.5, rows_per_batch=seq, name="ffn_down")
        u = _norm_mod(h, norm_gains[i, 1], md(3), md(4), rows_per_batch=seq)
        if is_attn:
            n_heads = attn_sinks.shape[1]
            n_qkv = w_mix.shape[1]
            n_kv = (n_qkv // HEAD_DIM - n_heads) // 2
            tn = n_kv * HEAD_DIM
            q_tiles = n_heads // n_kv
            ones = jnp.ones((1, HEAD_DIM), F32)
            gains = jnp.stack([attn_q_gain[jm].reshape(1, HEAD_DIM) * HEAD_DIM ** -0.5] * q_tiles
                              + [attn_k_gain[jm].reshape(1, HEAD_DIM), ones])
            qkv, = _mm(u, w_mix, [0], n_qkv,
                       [(gains, (1, 1, HEAD_DIM), lambda i_, j_, k_: (j_, 0, 0))], [BF16],
                       functools.partial(_ep_qkv, n_norm_tiles=q_tiles + 1),
                       name="attn_qkv", tm=1024, tn=tn)
            heads = jnp.arange(1, n_heads + 1, dtype=F32)
            slopes = jnp.exp2(-8.0 * heads / n_heads)
            y = _attention(qkv, slopes, attn_sinks[jm].astype(F32), batch=batch, seq=seq,
                           n_heads=n_heads, n_kv=n_kv)
            mixer_name = "attn_out"
        else:
            fd = (w_mix.shape[1] - 2 * d) // 2
            tn = 1024
            qh, = _mm(u, w_mix, [0], fd, [], [BF16], _ep_silu, name="hgrn_q", tn=tn)
            lbp = hgrn_lower_bounds.astype(F32)
            log_f, kh = _mm(u, w_mix, [fd // tn], fd,
                            [(lbp, (depth, tn), lambda i_, j_, k_: (0, j_))], [F32, BF16],
                            functools.partial(_ep_forget, layer=i), name="hgrn_f", tn=tn)
            vh, = _mm(u, w_mix, [2 * fd // tn], d, [], [BF16], _ep_cast, name="hgrn_i", tn=tn)
            gh, = _mm(u, w_mix, [(2 * fd + d) // tn], d, [], [BF16], _ep_silu, name="hgrn_g", tn=tn)
            y = _hgrn_recurrence(qh, kh, vh, log_f, gh, hgrn_g_gain[jm], batch=batch, seq=seq)
            mixer_name = "hgrn_out"
        h = _residual_mm(y, w_o, h, md(5), 1.0, rows_per_batch=seq, name=mixer_name)
        u = _norm_mod(h, norm_gains[i, 2], md(6), md(7), rows_per_batch=seq)
        later = [(ffn2_w_down, i)] + ([(ffn1_w_gu, i + 1)] if i + 1 < depth else [])
        act, w_down, *next_gu = _gate_up(u, w_gu, later)
        h = _residual_mm(act, w_down, h, md(8), 0.5, rows_per_batch=seq, name="ffn_down")
        if next_gu:
            w_gu, = next_gu
    return h.reshape(batch, seq, d)
```

```python
import functools

import numpy as np
import jax
import jax.numpy as jnp
from jax import lax
from jax.experimental import pallas as pl
from jax.experimental.pallas import tpu as pltpu

F32 = jnp.float32
BF16 = jnp.bfloat16
EPS = 1e-6
LOG2_E = 1.4426950408889634

WINDOW = 128
HEAD_DIM = 128
N_MOD = 9
HGRN_CHUNK = 128

V7X_VMEM_BYTES = 64 * 1024 * 1024
VMEM_HEADROOM_BYTES = 6 * 1024 * 1024
LANES = 128

TILE_M = 1024
TILE_N = 1024
FULL_K = 4096
SWIGLU_TILE_N = TILE_N // 2


def _vmem_limit(block_bytes):
    want = 2 * block_bytes + VMEM_HEADROOM_BYTES
    return int(min(max(want, 32 * 1024 * 1024), V7X_VMEM_BYTES - 2 * 1024 * 1024))


def _nbytes(shape, dtype):
    return int(np.prod(shape)) * jnp.dtype(dtype).itemsize


def _silu(x):
    half = 0.5 * x
    return half + half * jnp.tanh(half)


def _dot(a, b):
    return jnp.dot(a, b, preferred_element_type=F32)


def _dot_nt(a, b):
    return lax.dot_general(a, b, (((1,), (1,)), ((), ())), preferred_element_type=F32)


def _dot_tn(a, b):
    return lax.dot_general(a, b, (((0,), (0,)), ((), ())), preferred_element_type=F32)


def _ada_kernel(c_ref, w_ref, b_ref, o_ref):
    cond = _silu(c_ref[...]).astype(BF16)
    o_ref[0] = _dot(cond, w_ref[0].astype(BF16)) + b_ref[0]


def _ada(c_pad, ada_w, ada_b, *, tn=512):
    depth, d, nd = ada_w.shape
    rows = c_pad.shape[0]
    blk = _nbytes((d, tn), F32) + _nbytes((rows, tn), F32) + _nbytes((rows, d), F32)
    return pl.pallas_call(
        _ada_kernel,
        grid=(depth, nd // tn),
        in_specs=[
            pl.BlockSpec((rows, d), lambda l, j: (0, 0)),
            pl.BlockSpec((1, d, tn), lambda l, j: (l, 0, j)),
            pl.BlockSpec((1, 1, tn), lambda l, j: (l, 0, j)),
        ],
        out_specs=pl.BlockSpec((1, rows, tn), lambda l, j: (l, 0, j)),
        out_shape=jax.ShapeDtypeStruct((depth, rows, nd), F32),
        compiler_params=pltpu.CompilerParams(
            dimension_semantics=("arbitrary", "arbitrary"),
            vmem_limit_bytes=_vmem_limit(blk)),
        name="ada_mod",
    )(c_pad, ada_w, ada_b.reshape(depth, 1, nd))


def _norm_mod_kernel(h_ref, gain_ref, shift_ref, scale_ref, o_ref):
    h = h_ref[...]
    y = h * lax.rsqrt(jnp.mean(h * h, axis=-1, keepdims=True) + EPS) * gain_ref[...]
    o_ref[...] = (y * (1.0 + scale_ref[0]) + shift_ref[0]).astype(o_ref.dtype)


def _norm_mod(h, gain, shift, scale, *, rows_per_batch, tm=512):
    m, d = h.shape
    tpb = rows_per_batch // tm
    blk = _nbytes((tm, d), F32) + _nbytes((tm, d), BF16)
    return pl.pallas_call(
        _norm_mod_kernel,
        grid=(m // tm,),
        in_specs=[
            pl.BlockSpec((tm, d), lambda i: (i, 0)),
            pl.BlockSpec((1, d), lambda i: (0, 0)),
            pl.BlockSpec((1, 1, d), lambda i: (i // tpb, 0, 0)),
            pl.BlockSpec((1, 1, d), lambda i: (i // tpb, 0, 0)),
        ],
        out_specs=pl.BlockSpec((tm, d), lambda i: (i, 0)),
        out_shape=jax.ShapeDtypeStruct((m, d), BF16),
        compiler_params=pltpu.CompilerParams(
            dimension_semantics=("arbitrary",),
            vmem_limit_bytes=_vmem_limit(blk)),
        name="norm_mod",
    )(h, gain.reshape(1, d), shift, scale)


def _mm_kernel(*refs, n_w, n_e, n_o, nk, n_cast, epilogue):
    x_ref = refs[0]
    w_refs = refs[1:1 + n_w]
    e_refs = refs[1 + n_w:1 + n_w + n_e]
    n_in = 1 + n_w + n_e + n_cast
    o_refs = refs[n_in:n_in + n_o]
    acc_refs = refs[n_in + n_o + n_cast:]
    j = pl.program_id(1)

    for src_ref, dst_ref in zip(refs[n_in - n_cast:n_in], refs[n_in + n_o:n_in + n_o + n_cast]):
        dst_ref[...] = src_ref[...].astype(dst_ref.dtype)

    def dots():
        x = x_ref[...]
        return [_dot(x, w[...]) for w in w_refs]

    if nk == 1:
        epilogue(dots(), j, e_refs, o_refs)
        return

    k = pl.program_id(2)

    @pl.when(k == 0)
    def _():
        for a, d in zip(acc_refs, dots()):
            a[...] = d

    if nk > 2:
        @pl.when((k > 0) & (k < nk - 1))
        def _():
            for a, d in zip(acc_refs, dots()):
                a[...] += d

    @pl.when(k == nk - 1)
    def _():
        epilogue([a[...] + d for a, d in zip(acc_refs, dots())], j, e_refs, o_refs)


def _cast_chunk_rows(n_rows, n_steps):
    rows = 16
    while n_rows % rows or n_rows // rows > n_steps:
        rows += 16
    return rows


def _mm(x, w, w_col_blocks, n_cols, extras, outs, epilogue, *, name, casts=(),
        tm=TILE_M, tn=TILE_N, tk=FULL_K):
    m, kdim = x.shape
    tk = min(tk, kdim)
    nk = kdim // tk
    grid = (m // tm, n_cols // tn, nk)
    in_specs = [pl.BlockSpec((tm, tk), lambda i, j, k: (i, k))]
    blk = _nbytes((tm, tk), x.dtype)
    for off in w_col_blocks:
        in_specs.append(pl.BlockSpec((tk, tn), lambda i, j, k, off=off: (k, j + off)))
        blk += _nbytes((tk, tn), w.dtype)
    for arr, bshape, imap in extras:
        in_specs.append(pl.BlockSpec(bshape, imap))
        blk += _nbytes(bshape, arr.dtype)
    out_specs = [pl.BlockSpec((tm, tn), lambda i, j, k: (i, j)) for _ in outs]
    out_shape = [jax.ShapeDtypeStruct((m, n_cols), dt) for dt in outs]
    blk += sum(_nbytes((tm, tn), dt) for dt in outs)
    operands = [x, *([w] * len(w_col_blocks)), *[e[0] for e in extras]]
    for src, layer in casts:
        _, n_rows, n_src_cols = src.shape
        rows = _cast_chunk_rows(n_rows, grid[0] * grid[1] * grid[2])
        chunk = lambda i, j, k, last=n_rows // rows - 1: jnp.minimum(
            (i * grid[1] + j) * grid[2] + k, last)
        in_specs.append(pl.BlockSpec((None, rows, n_src_cols),
                                     lambda i, j, k, layer=layer, chunk=chunk: (layer, chunk(i, j, k), 0)))
        out_specs.append(pl.BlockSpec((rows, n_src_cols), lambda i, j, k, chunk=chunk: (chunk(i, j, k), 0)))
        out_shape.append(jax.ShapeDtypeStruct((n_rows, n_src_cols), BF16))
        operands.append(src)
        blk += _nbytes((rows, n_src_cols), F32) + _nbytes((rows, n_src_cols), BF16)
    n_w = len(w_col_blocks)
    scratch = [pltpu.VMEM((tm, tn), F32) for _ in range(n_w)] if nk > 1 else []
    scratch_bytes = (len(scratch) + 3 * n_w) * _nbytes((tm, tn), F32)
    kern = functools.partial(_mm_kernel, n_w=n_w, n_e=len(extras), n_o=len(outs), nk=nk,
                             n_cast=len(casts), epilogue=epilogue)
    return pl.pallas_call(
        kern,
        grid=grid,
        in_specs=in_specs,
        out_specs=out_specs,
        out_shape=out_shape,
        scratch_shapes=scratch,
        compiler_params=pltpu.CompilerParams(
            dimension_semantics=("arbitrary", "arbitrary", "arbitrary"),
            vmem_limit_bytes=_vmem_limit(blk + scratch_bytes // 2)),
        name=name,
    )(*operands)


def _cast_kernel(src_ref, o_ref):
    o_ref[...] = src_ref[...].astype(o_ref.dtype)


def _cast_bf16(src, layer, *, rows=128):
    _, n_rows, n_cols = src.shape
    blk = _nbytes((rows, n_cols), F32) + _nbytes((rows, n_cols), BF16)
    return pl.pallas_call(
        _cast_kernel,
        grid=(n_rows // rows,),
        in_specs=[pl.BlockSpec((None, rows, n_cols), lambda r: (layer, r, 0))],
        out_specs=pl.BlockSpec((rows, n_cols), lambda r: (r, 0)),
        out_shape=jax.ShapeDtypeStruct((n_rows, n_cols), BF16),
        compiler_params=pltpu.CompilerParams(
            dimension_semantics=("arbitrary",),
            vmem_limit_bytes=_vmem_limit(blk)),
        name="cast_weights",
    )(src)


def _ep_swiglu(accs, j, e_refs, o_refs):
    a, b = accs
    o_refs[0][...] = (_silu(a) * b).astype(o_refs[0].dtype)


def _ep_residual(accs, j, e_refs, o_refs, *, coef):
    h_ref, gate_ref = e_refs
    o_refs[0][...] = h_ref[...] + (coef * gate_ref[0]) * accs[0]


def _ep_qkv(accs, j, e_refs, o_refs, *, n_norm_tiles):
    y = accs[0]
    gain_ref, = e_refs
    o_ref = o_refs[0]
    tn = y.shape[1]

    @pl.when(j < n_norm_tiles)
    def _():
        g = gain_ref[0]
        for hd in range(tn // HEAD_DIM):
            yh = y[:, hd * HEAD_DIM:(hd + 1) * HEAD_DIM]
            inv = lax.rsqrt(jnp.mean(yh * yh, axis=-1, keepdims=True) + EPS)
            o_ref[:, hd * HEAD_DIM:(hd + 1) * HEAD_DIM] = (yh * inv * g).astype(o_ref.dtype)

    @pl.when(j >= n_norm_tiles)
    def _():
        o_ref[...] = y.astype(o_ref.dtype)


def _ep_silu(accs, j, e_refs, o_refs):
    o_refs[0][...] = _silu(accs[0]).astype(o_refs[0].dtype)


def _ep_cast(accs, j, e_refs, o_refs):
    o_refs[0][...] = accs[0].astype(o_refs[0].dtype)


def _ep_forget(accs, j, e_refs, o_refs, *, layer):
    f = accs[0]
    lbp = e_refs[0][...]
    depth = lbp.shape[0]
    mx = lbp[0:1]
    for l in range(1, depth):
        mx = jnp.maximum(mx, lbp[l:l + 1])
    es = [jnp.exp(lbp[l:l + 1] - mx) for l in range(depth)]
    tot = es[0]
    for l in range(1, depth):
        tot = tot + es[l]
    s = [e / tot for e in es]
    cum = s[0]
    for l in range(1, layer + 1):
        cum = cum + s[l]
    lb = cum - s[0]
    a2 = jnp.log(lb) * LOG2_E
    c2 = jnp.log1p(-lb) * LOG2_E
    f2 = f * LOG2_E
    t = jnp.exp2(-jnp.abs(f2))
    one_t = 1.0 + t
    b2 = (c2 + jnp.minimum(f2, 0.0)) - jnp.log(one_t) * LOG2_E
    e = jnp.exp2(-jnp.abs(a2 - b2))
    o_refs[0][...] = jnp.maximum(a2, b2) + jnp.log(1.0 + e) * LOG2_E
    o_refs[1][...] = ((1.0 - lb) * jnp.where(f >= 0.0, t, 1.0) / one_t).astype(o_refs[1].dtype)


def _split3(x):
    hi = x.astype(BF16)
    r = x - hi.astype(F32)
    mid = r.astype(BF16)
    lo = (r - mid.astype(F32)).astype(BF16)
    return hi, mid, lo


def _attn_tables(sinks, *, n_kv, group, blk):
    n_heads = n_kv * group
    q_feat = jnp.zeros((n_heads, blk, LANES), BF16)
    for i, s_i in enumerate(_split3(sinks)):
        q_feat = q_feat.at[:, :, i].set(jnp.broadcast_to(s_i[:, None], (n_heads, blk)))
    q_feat = q_feat.reshape(n_kv, group * blk, LANES)
    k_feat = np.zeros((2 * blk, LANES), np.float32)
    k_feat[0, 0:3] = 1.0
    qi = np.arange(blk)[:, None]
    kj = np.arange(2 * blk)[None, :]
    dist = (qi + blk - kj).astype(np.float32)
    valid = (dist >= 0) & (dist < WINDOW)
    sink_col = kj == 0
    inf = np.float32(np.inf)
    dist_all = np.where(sink_col, np.float32(0), np.where(valid, dist, inf))
    dist_first = np.where(sink_col, np.float32(0), np.where(valid & (kj >= blk), dist, inf))
    return q_feat, jnp.asarray(k_feat, BF16), jnp.asarray(dist_all), jnp.asarray(dist_first)


def _attn_kernel(slope_ref, qf_ref, kf_ref, dist_ref, dist_first_ref, q_ref, kc_ref, kp_ref, vc_ref,
                 vp_ref, o_ref, d_scr, *, n_kv, group):
    blk = q_ref.shape[0]
    d_scr[...] = jnp.where(pl.program_id(1) > 0, dist_ref[...], dist_first_ref[...])
    keep = (lax.broadcasted_iota(jnp.int32, (16, HEAD_DIM), 0) != 0).astype(BF16)
    ones = jnp.ones((2 * blk, HEAD_DIM), BF16)
    k_feat = kf_ref[...]

    def kv_head(hk, carry):
        ko = pl.ds(pl.multiple_of(hk * HEAD_DIM, HEAD_DIM), HEAD_DIM)
        kp = kp_ref[:, ko]
        vp = vp_ref[:, ko]
        vp = jnp.concatenate([vp[:16] * keep, vp[16:]], axis=0)
        kp = jnp.concatenate([kp[:16] * keep, kp[16:]], axis=0)
        k2 = jnp.concatenate([jnp.concatenate([kp, kc_ref[:, ko]], axis=0), k_feat], axis=1)
        v2 = jnp.concatenate([jnp.concatenate([vp, vc_ref[:, ko]], axis=0), ones], axis=1)
        cols = [pl.ds(pl.multiple_of((hk * group + g) * HEAD_DIM, HEAD_DIM), HEAD_DIM)
                for g in range(group)]
        q4 = jnp.concatenate([q_ref[:, cg] for cg in cols], axis=0)
        q4 = jnp.concatenate([q4, qf_ref[hk]], axis=1)
        s = _dot_nt(q4, k2)
        dist = d_scr[...]
        logits = jnp.concatenate(
            [s[g * blk:(g + 1) * blk] - slope_ref[hk * group + g] * dist for g in range(group)],
            axis=0)
        m = jnp.max(logits, axis=-1, keepdims=True)
        p = jnp.exp(logits - m).astype(BF16)
        o = _dot(p, v2)
        out = o[:, :HEAD_DIM] / o[:, HEAD_DIM:]
        for g, cg in enumerate(cols):
            o_ref[:, cg] = out[g * blk:(g + 1) * blk].astype(o_ref.dtype)
        return carry

    lax.fori_loop(0, n_kv, kv_head, 0, unroll=True)


def _attention(qkv, slopes, sinks, *, batch, seq, n_heads, n_kv):
    m = qkv.shape[0]
    blk = WINDOW
    group = n_heads // n_kv
    nb = seq // blk
    dq = n_heads * HEAD_DIM
    dkv = n_kv * HEAD_DIM
    k_col = dq // dkv
    v_col = k_col + 1
    q_feat, k_feat, dist, dist_first = _attn_tables(sinks, n_kv=n_kv, group=group, blk=blk)
    cur = lambda b, n: b * nb + n
    prev = lambda b, n: b * nb + jnp.maximum(n - 1, 0)
    const2 = lambda b, n: (0, 0)
    blk_bytes = (2 * _nbytes((blk, dq), BF16) + 4 * _nbytes((blk, dkv), BF16)
                 + _nbytes(q_feat.shape, BF16) + 3 * _nbytes(dist.shape, F32))
    return pl.pallas_call(
        functools.partial(_attn_kernel, n_kv=n_kv, group=group),
        grid=(batch, nb),
        in_specs=[
            pl.BlockSpec(memory_space=pltpu.SMEM),
            pl.BlockSpec(q_feat.shape, lambda b, n: (0, 0, 0)),
            pl.BlockSpec(k_feat.shape, const2),
            pl.BlockSpec(dist.shape, const2),
            pl.BlockSpec(dist.shape, const2),
            pl.BlockSpec((blk, dq), lambda b, n: (cur(b, n), 0)),
            pl.BlockSpec((blk, dkv), lambda b, n: (cur(b, n), k_col)),
            pl.BlockSpec((blk, dkv), lambda b, n: (prev(b, n), k_col)),
            pl.BlockSpec((blk, dkv), lambda b, n: (cur(b, n), v_col)),
            pl.BlockSpec((blk, dkv), lambda b, n: (prev(b, n), v_col)),
        ],
        out_specs=pl.BlockSpec((blk, dq), lambda b, n: (cur(b, n), 0)),
        out_shape=jax.ShapeDtypeStruct((m, dq), BF16),
        scratch_shapes=[pltpu.VMEM(dist.shape, F32)],
        compiler_params=pltpu.CompilerParams(
            dimension_semantics=("arbitrary", "arbitrary"),
            vmem_limit_bytes=_vmem_limit(blk_bytes)),
        name="swa_attention",
    )(slopes, q_feat, k_feat, dist, dist_first, qkv, qkv, qkv, qkv, qkv)


def _hgrn_tables(c):
    halves = [c >> (lv + 1) for lv in range(int(np.log2(c)))]
    r = np.arange(c)[:, None]
    j = np.arange(c)[None, :]

    def partial_sums(half):
        ref = (r // (2 * half)) * (2 * half) + half - 1
        return np.where(r > ref, (j > ref) & (j <= r), (j > r) & (j <= ref))

    masks = [((r // (2 * h)) == (j // (2 * h))) & ((r % (2 * h)) >= h) & ((j % (2 * h)) < h)
             for h in halves]
    masks.append(r == j)
    w = np.concatenate([j <= r, partial_sums(4), partial_sums(2)], axis=0).astype(np.float32)
    w3 = np.concatenate([w, w, w], axis=1)
    return jnp.asarray(w3, BF16), jnp.asarray(np.stack(masks).astype(np.float32), BF16), halves


def _hgrn_kernel(w3_ref, mask_ref, gain_ref, q_ref, k_ref, v_ref, lf_ref, gate_ref, o_ref,
                 s_ref, d_scr, *, c, halves):
    @pl.when(pl.program_id(2) == 0)
    def _():
        s_ref[...] = jnp.zeros_like(s_ref)

    n_chunks = q_ref.shape[0] // c
    zero_h = jnp.zeros((c, LANES), BF16)
    n_lv = len(halves)

    def block_diag(x0, x1):
        return jnp.concatenate([jnp.concatenate([x0, zero_h], axis=1),
                                jnp.concatenate([zero_h, x1], axis=1)], axis=0)

    def both(x):
        return x[:, :LANES], x[:, LANES:]

    def pair_mask(i):
        mk = mask_ref[i]
        return jnp.concatenate([mk, mk], axis=1)

    def chunk(ci, carry):
        rows = pl.ds(pl.multiple_of(ci * c, c), c)
        q = lambda: q_ref[rows, :]
        k = lambda: k_ref[rows, :]
        lf3 = jnp.concatenate(_split3(lf_ref[rows, :]), axis=0)
        d_scr[...] = _dot(w3_ref[...], lf3)
        b = lambda: d_scr[0:c, :]

        def level_exponent(half):
            pieces = []
            for g0 in range(0, c, 2 * half):
                ref = jnp.broadcast_to(d_scr[g0 + half - 1:g0 + half, :], (half, 2 * LANES))
                pieces += [ref - d_scr[g0:g0 + half, :], d_scr[g0 + half:g0 + 2 * half, :] - ref]
            return jnp.concatenate(pieces, axis=0)

        e1 = jnp.exp2(lf_ref[rows, :]).astype(BF16)
        p = _dot_nt(jnp.concatenate([q(), q() * e1], axis=0), block_diag(*both(k())))
        a = p[0:c].astype(BF16) * pair_mask(n_lv) + p[c:2 * c].astype(BF16) * pair_mask(n_lv - 1)
        for li, half in enumerate(halves[:-1]):
            if half >= 8:
                dn = level_exponent(half)
            else:
                dn = d_scr[c:2 * c, :] if half == 4 else d_scr[2 * c:3 * c, :]
            e = jnp.exp2(dn).astype(BF16)
            a = a + _dot_nt(q() * e, block_diag(*both(k() * e))).astype(BF16) * pair_mask(li)
        intra = _dot(a, block_diag(*both(v_ref[rows, :])))

        s0 = s_ref[0]
        s1 = s_ref[1]
        inter = _dot_nt(q() * jnp.exp2(b()).astype(BF16),
                        block_diag(s0.astype(BF16), s1.astype(BF16)))
        o = inter + intra

        b_last = d_scr[c - 1:c, :]
        u = _dot_tn(v_ref[rows, :], k() * jnp.exp2(b_last - b()).astype(BF16))
        decay = jnp.exp2(b_last)
        s_ref[0] = s0 * decay[:, :LANES] + u[:LANES, :LANES]
        s_ref[1] = s1 * decay[:, LANES:] + u[LANES:, LANES:]

        gate = gate_ref[rows, :].astype(F32)
        gain = gain_ref[...]
        for hh in range(2):
            lanes = slice(hh * LANES, (hh + 1) * LANES)
            oh = o[:, lanes]
            y = oh * lax.rsqrt(jnp.mean(oh * oh, axis=-1, keepdims=True) + EPS) * gain
            o_ref[rows, lanes] = (y * gate[:, lanes]).astype(o_ref.dtype)
        return carry

    lax.fori_loop(0, n_chunks, chunk, 0, unroll=min(n_chunks, 16))


def _hgrn_recurrence(q, k, v, log_f, gate, gain, *, batch, seq, tb=4096):
    m, d = q.shape
    c = HGRN_CHUNK
    w3, masks, halves = _hgrn_tables(c)
    tb = min(tb, seq)
    nt = seq // tb
    width = 2 * HEAD_DIM
    tile = pl.BlockSpec((tb, width), lambda b, h, t: (b * nt + t, h))
    const2 = lambda b, h, t: (0, 0)
    blk = (_nbytes((tb, width), F32) + 5 * _nbytes((tb, width), BF16)
           + _nbytes(w3.shape, BF16) + _nbytes(masks.shape, BF16))
    return pl.pallas_call(
        functools.partial(_hgrn_kernel, c=c, halves=halves),
        grid=(batch, d // width, nt),
        in_specs=[
            pl.BlockSpec(w3.shape, const2),
            pl.BlockSpec(masks.shape, lambda b, h, t: (0, 0, 0)),
            pl.BlockSpec((1, HEAD_DIM), const2),
            tile, tile, tile, tile, tile,
        ],
        out_specs=tile,
        out_shape=jax.ShapeDtypeStruct((m, d), BF16),
        scratch_shapes=[pltpu.VMEM((2, HEAD_DIM, HEAD_DIM), F32), pltpu.VMEM((3 * c, width), F32)],
        compiler_params=pltpu.CompilerParams(
            dimension_semantics=("arbitrary", "arbitrary", "arbitrary"),
            vmem_limit_bytes=_vmem_limit(blk)),
        name="hgrn2_recurrence",
    )(w3, masks, gain.reshape(1, HEAD_DIM), q, k, v, log_f, gate)


def _bcast_mod(mod_l, idx, batch):
    d = mod_l.shape[-1] // N_MOD
    return mod_l[:batch, idx * d:(idx + 1) * d].reshape(batch, 1, d)


def _gate_up(u, w_gu, casts):
    f = w_gu.shape[1] // 2
    tn = SWIGLU_TILE_N
    return _mm(u, w_gu, [0, f // tn], f, [], [BF16], _ep_swiglu, name="ffn_gate_up",
               casts=casts, tn=tn)


def _residual_mm(x, w, h, gate, coef, *, rows_per_batch, name):
    tm, tn = TILE_M, TILE_N
    tpb = rows_per_batch // tm
    extras = [
        (h, (tm, tn), lambda i, j, k: (i, j)),
        (gate, (1, 1, tn), lambda i, j, k: (i // tpb, 0, j)),
    ]
    tk = FULL_K if x.shape[1] <= FULL_K else FULL_K // 2
    out, = _mm(x, w, [0], w.shape[1], extras, [F32], functools.partial(_ep_residual, coef=coef),
               name=name, tm=tm, tn=tn, tk=tk)
    return out


def kernel(x, c, ada_w, ada_b, norm_gains, ffn1_w_gu, ffn1_w_down, ffn2_w_gu, ffn2_w_down,
           attn_w_qkv, attn_w_o, attn_q_gain, attn_k_gain, attn_sinks,
           hgrn_w_qfig, hgrn_w_o, hgrn_g_gain, hgrn_lower_bounds):
    batch, seq, d = x.shape
    depth = ada_w.shape[0]
    m = batch * seq
    n_mixers = 2

    c_pad = jnp.zeros((8, d), F32).at[:batch].set(c)
    mod = _ada(c_pad, ada_w, ada_b)

    w_gu = _cast_bf16(ffn1_w_gu, 0)
    h = x.reshape(m, d)
    for i in range(depth):
        md = lambda idx: _bcast_mod(mod[i], idx, batch)
        jm = i // n_mixers
        is_attn = i % n_mixers == 0
        u = _norm_mod(h, norm_gains[i, 0], md(0), md(1), rows_per_batch=seq)
        mixer_w = (attn_w_qkv, attn_w_o) if is_attn else (hgrn_w_qfig, hgrn_w_o)
        act, w_down, w_mix, w_o, w_gu = _gate_up(
            u, w_gu, [(ffn1_w_down, i), (mixer_w[0], jm), (mixer_w[1], jm), (ffn2_w_gu, i)])
        h = _residual_mm(act, w_down, h, md(2), 0.5, rows_per_batch=seq, name="ffn_down")
        u = _norm_mod(h, norm_gains[i, 1], md(3), md(4), rows_per_batch=seq)
        if is_attn:
            n_heads = attn_sinks.shape[1]
            n_qkv = w_mix.shape[1]
            n_kv = (n_qkv // HEAD_DIM - n_heads) // 2
            tn = n_kv * HEAD_DIM
            q_tiles = n_heads // n_kv
            ones = jnp.ones((1, HEAD_DIM), F32)
            gains = jnp.stack([attn_q_gain[jm].reshape(1, HEAD_DIM) * HEAD_DIM ** -0.5] * q_tiles
                              + [attn_k_gain[jm].reshape(1, HEAD_DIM), ones])
            qkv, = _mm(u, w_mix, [0], n_qkv,
                       [(gains, (1, 1, HEAD_DIM), lambda i_, j_, k_: (j_, 0, 0))], [BF16],
                       functools.partial(_ep_qkv, n_norm_tiles=q_tiles + 1),
                       name="attn_qkv", tn=tn)
            heads = jnp.arange(1, n_heads + 1, dtype=F32)
            slopes = jnp.exp2(-8.0 * heads / n_heads)
            y = _attention(qkv, slopes, attn_sinks[jm].astype(F32), batch=batch, seq=seq,
                           n_heads=n_heads, n_kv=n_kv)
            mixer_name = "attn_out"
        else:
            fd = (w_mix.shape[1] - 2 * d) // 2
            tn = TILE_N
            qh, = _mm(u, w_mix, [0], fd, [], [BF16], _ep_silu, name="hgrn_q", tn=tn)
            lbp = hgrn_lower_bounds.astype(F32)
            log_f, kh = _mm(u, w_mix, [fd // tn], fd,
                            [(lbp, (depth, tn), lambda i_, j_, k_: (0, j_))], [F32, BF16],
                            functools.partial(_ep_forget, layer=i), name="hgrn_f", tn=tn)
            vh, = _mm(u, w_mix, [2 * fd // tn], d, [], [BF16], _ep_cast, name="hgrn_i", tn=tn)
            gh, = _mm(u, w_mix, [(2 * fd + d) // tn], d, [], [BF16], _ep_silu, name="hgrn_g", tn=tn)
            y = _hgrn_recurrence(qh, kh, vh, log_f, gh, hgrn_g_gain[jm], batch=batch, seq=seq)
            mixer_name = "hgrn_out"
        h = _residual_mm(y, w_o, h, md(5), 1.0, rows_per_batch=seq, name=mixer_name)
        u = _norm_mod(h, norm_gains[i, 2], md(6), md(7), rows_per_batch=seq)
        later = [(ffn2_w_down, i)] + ([(ffn1_w_gu, i + 1)] if i + 1 < depth else [])
        act, w_down, *next_gu = _gate_up(u, w_gu, later)
        h = _residual_mm(act, w_down, h, md(8), 0.5, rows_per_batch=seq, name="ffn_down")
        if next_gu:
            w_gu, = next_gu
    return h.reshape(batch, seq, d)
```

```python
import functools

import numpy as np
import jax
import jax.numpy as jnp
from jax import lax
from jax.experimental import pallas as pl
from jax.experimental.pallas import tpu as pltpu

F32 = jnp.float32
BF16 = jnp.bfloat16
EPS = 1e-6
LOG2_E = 1.4426950408889634

WINDOW = 128
HEAD_DIM = 128
N_MOD = 9
HGRN_CHUNK = 128

V7X_VMEM_BYTES = 64 * 1024 * 1024
VMEM_HEADROOM_BYTES = 6 * 1024 * 1024
LANES = 128

TILE_M = 1024
TILE_N = 1024
FULL_K = 4096
SWIGLU_TILE_N = TILE_N // 2


def _vmem_limit(block_bytes):
    want = 2 * block_bytes + VMEM_HEADROOM_BYTES
    return int(min(max(want, 32 * 1024 * 1024), V7X_VMEM_BYTES - 2 * 1024 * 1024))


def _nbytes(shape, dtype):
    return int(np.prod(shape)) * jnp.dtype(dtype).itemsize


def _silu(x):
    half = 0.5 * x
    return half + half * jnp.tanh(half)


def _dot(a, b):
    return jnp.dot(a, b, preferred_element_type=F32)


def _dot_nt(a, b):
    return lax.dot_general(a, b, (((1,), (1,)), ((), ())), preferred_element_type=F32)


def _dot_tn(a, b):
    return lax.dot_general(a, b, (((0,), (0,)), ((), ())), preferred_element_type=F32)


def _ada_kernel(c_ref, w_ref, b_ref, o_ref):
    cond = _silu(c_ref[...]).astype(BF16)
    o_ref[0] = _dot(cond, w_ref[0].astype(BF16)) + b_ref[0]


def _ada(c_pad, ada_w, ada_b, *, tn=512):
    depth, d, nd = ada_w.shape
    rows = c_pad.shape[0]
    blk = _nbytes((d, tn), F32) + _nbytes((rows, tn), F32) + _nbytes((rows, d), F32)
    return pl.pallas_call(
        _ada_kernel,
        grid=(depth, nd // tn),
        in_specs=[
            pl.BlockSpec((rows, d), lambda l, j: (0, 0)),
            pl.BlockSpec((1, d, tn), lambda l, j: (l, 0, j)),
            pl.BlockSpec((1, 1, tn), lambda l, j: (l, 0, j)),
        ],
        out_specs=pl.BlockSpec((1, rows, tn), lambda l, j: (l, 0, j)),
        out_shape=jax.ShapeDtypeStruct((depth, rows, nd), F32),
        compiler_params=pltpu.CompilerParams(
            dimension_semantics=("arbitrary", "arbitrary"),
            vmem_limit_bytes=_vmem_limit(blk)),
        name="ada_mod",
    )(c_pad, ada_w, ada_b.reshape(depth, 1, nd))


def _norm_mod_kernel(h_ref, gain_ref, shift_ref, scale_ref, o_ref):
    h = h_ref[...]
    y = h * lax.rsqrt(jnp.mean(h * h, axis=-1, keepdims=True) + EPS) * gain_ref[...]
    o_ref[...] = (y * (1.0 + scale_ref[0]) + shift_ref[0]).astype(o_ref.dtype)


def _norm_mod(h, gain, shift, scale, *, rows_per_batch, tm=512):
    m, d = h.shape
    tpb = rows_per_batch // tm
    blk = _nbytes((tm, d), F32) + _nbytes((tm, d), BF16)
    return pl.pallas_call(
        _norm_mod_kernel,
        grid=(m // tm,),
        in_specs=[
            pl.BlockSpec((tm, d), lambda i: (i, 0)),
            pl.BlockSpec((1, d), lambda i: (0, 0)),
            pl.BlockSpec((1, 1, d), lambda i: (i // tpb, 0, 0)),
            pl.BlockSpec((1, 1, d), lambda i: (i // tpb, 0, 0)),
        ],
        out_specs=pl.BlockSpec((tm, d), lambda i: (i, 0)),
        out_shape=jax.ShapeDtypeStruct((m, d), BF16),
        compiler_params=pltpu.CompilerParams(
            dimension_semantics=("arbitrary",),
            vmem_limit_bytes=_vmem_limit(blk)),
        name="norm_mod",
    )(h, gain.reshape(1, d), shift, scale)


def _mm_kernel(*refs, n_w, n_e, n_o, nk, n_cast, epilogue):
    x_ref = refs[0]
    w_refs = refs[1:1 + n_w]
    e_refs = refs[1 + n_w:1 + n_w + n_e]
    n_in = 1 + n_w + n_e + n_cast
    o_refs = refs[n_in:n_in + n_o]
    acc_refs = refs[n_in + n_o + n_cast:]
    j = pl.program_id(1)

    for src_ref, dst_ref in zip(refs[n_in - n_cast:n_in], refs[n_in + n_o:n_in + n_o + n_cast]):
        dst_ref[...] = src_ref[...].astype(dst_ref.dtype)

    def dots():
        x = x_ref[...]
        return [_dot(x, w[...]) for w in w_refs]

    if nk == 1:
        epilogue(dots(), j, e_refs, o_refs)
        return

    k = pl.program_id(2)

    @pl.when(k == 0)
    def _():
        for a, d in zip(acc_refs, dots()):
            a[...] = d

    if nk > 2:
        @pl.when((k > 0) & (k < nk - 1))
        def _():
            for a, d in zip(acc_refs, dots()):
                a[...] += d

    @pl.when(k == nk - 1)
    def _():
        epilogue([a[...] + d for a, d in zip(acc_refs, dots())], j, e_refs, o_refs)


def _cast_chunk_rows(n_rows, n_steps):
    rows = 16
    while n_rows % rows or n_rows // rows > n_steps:
        rows += 16
    return rows


def _mm(x, w, w_col_blocks, n_cols, extras, outs, epilogue, *, name, casts=(), k_slice=None,
        tm=TILE_M, tn=TILE_N, tk=FULL_K):
    m, kdim = x.shape
    tk = min(tk, kdim)
    k0, nk = k_slice if k_slice is not None else (0, kdim // tk)
    grid = (m // tm, n_cols // tn, nk)
    in_specs = [pl.BlockSpec((tm, tk), lambda i, j, k: (i, k + k0))]
    blk = _nbytes((tm, tk), x.dtype)
    for off in w_col_blocks:
        in_specs.append(pl.BlockSpec((tk, tn), lambda i, j, k, off=off: (k + k0, j + off)))
        blk += _nbytes((tk, tn), w.dtype)
    for arr, bshape, imap in extras:
        in_specs.append(pl.BlockSpec(bshape, imap))
        blk += _nbytes(bshape, arr.dtype)
    out_specs = [pl.BlockSpec((tm, tn), lambda i, j, k: (i, j)) for _ in outs]
    out_shape = [jax.ShapeDtypeStruct((m, n_cols), dt) for dt in outs]
    blk += sum(_nbytes((tm, tn), dt) for dt in outs)
    operands = [x, *([w] * len(w_col_blocks)), *[e[0] for e in extras]]
    for src, layer in casts:
        _, n_rows, n_src_cols = src.shape
        rows = _cast_chunk_rows(n_rows, grid[0] * grid[1] * grid[2])
        chunk = lambda i, j, k, last=n_rows // rows - 1: jnp.minimum(
            (i * grid[1] + j) * grid[2] + k, last)
        in_specs.append(pl.BlockSpec((None, rows, n_src_cols),
                                     lambda i, j, k, layer=layer, chunk=chunk: (layer, chunk(i, j, k), 0)))
        out_specs.append(pl.BlockSpec((rows, n_src_cols), lambda i, j, k, chunk=chunk: (chunk(i, j, k), 0)))
        out_shape.append(jax.ShapeDtypeStruct((n_rows, n_src_cols), BF16))
        operands.append(src)
        blk += _nbytes((rows, n_src_cols), F32) + _nbytes((rows, n_src_cols), BF16)
    n_w = len(w_col_blocks)
    scratch = [pltpu.VMEM((tm, tn), F32) for _ in range(n_w)] if nk > 1 else []
    scratch_bytes = (len(scratch) + 3 * n_w) * _nbytes((tm, tn), F32)
    kern = functools.partial(_mm_kernel, n_w=n_w, n_e=len(extras), n_o=len(outs), nk=nk,
                             n_cast=len(casts), epilogue=epilogue)
    return pl.pallas_call(
        kern,
        grid=grid,
        in_specs=in_specs,
        out_specs=out_specs,
        out_shape=out_shape,
        scratch_shapes=scratch,
        compiler_params=pltpu.CompilerParams(
            dimension_semantics=("arbitrary", "arbitrary", "arbitrary"),
            vmem_limit_bytes=_vmem_limit(blk + scratch_bytes // 2)),
        name=name,
    )(*operands)


def _cast_kernel(src_ref, o_ref):
    o_ref[...] = src_ref[...].astype(o_ref.dtype)


def _cast_bf16(src, layer, *, rows=128):
    _, n_rows, n_cols = src.shape
    blk = _nbytes((rows, n_cols), F32) + _nbytes((rows, n_cols), BF16)
    return pl.pallas_call(
        _cast_kernel,
        grid=(n_rows // rows,),
        in_specs=[pl.BlockSpec((None, rows, n_cols), lambda r: (layer, r, 0))],
        out_specs=pl.BlockSpec((rows, n_cols), lambda r: (r, 0)),
        out_shape=jax.ShapeDtypeStruct((n_rows, n_cols), BF16),
        compiler_params=pltpu.CompilerParams(
            dimension_semantics=("arbitrary",),
            vmem_limit_bytes=_vmem_limit(blk)),
        name="cast_weights",
    )(src)


def _ep_swiglu(accs, j, e_refs, o_refs):
    a, b = accs
    o_refs[0][...] = (_silu(a) * b).astype(o_refs[0].dtype)


def _ep_residual(accs, j, e_refs, o_refs, *, coef):
    h_ref, gate_ref = e_refs
    o_refs[0][...] = h_ref[...] + (coef * gate_ref[0]) * accs[0]


def _ep_qkv(accs, j, e_refs, o_refs, *, n_norm_tiles):
    y = accs[0]
    gain_ref, = e_refs
    o_ref = o_refs[0]
    tn = y.shape[1]

    @pl.when(j < n_norm_tiles)
    def _():
        g = gain_ref[0]
        for hd in range(tn // HEAD_DIM):
            yh = y[:, hd * HEAD_DIM:(hd + 1) * HEAD_DIM]
            inv = lax.rsqrt(jnp.mean(yh * yh, axis=-1, keepdims=True) + EPS)
            o_ref[:, hd * HEAD_DIM:(hd + 1) * HEAD_DIM] = (yh * inv * g).astype(o_ref.dtype)

    @pl.when(j >= n_norm_tiles)
    def _():
        o_ref[...] = y.astype(o_ref.dtype)


def _ep_silu(accs, j, e_refs, o_refs):
    o_refs[0][...] = _silu(accs[0]).astype(o_refs[0].dtype)


def _ep_cast(accs, j, e_refs, o_refs):
    o_refs[0][...] = accs[0].astype(o_refs[0].dtype)


def _ep_forget(accs, j, e_refs, o_refs, *, layer):
    f = accs[0]
    lbp = e_refs[0][...]
    depth = lbp.shape[0]
    mx = lbp[0:1]
    for l in range(1, depth):
        mx = jnp.maximum(mx, lbp[l:l + 1])
    es = [jnp.exp(lbp[l:l + 1] - mx) for l in range(depth)]
    tot = es[0]
    for l in range(1, depth):
        tot = tot + es[l]
    s = [e / tot for e in es]
    cum = s[0]
    for l in range(1, layer + 1):
        cum = cum + s[l]
    lb = cum - s[0]
    a2 = jnp.log(lb) * LOG2_E
    c2 = jnp.log1p(-lb) * LOG2_E
    f2 = f * LOG2_E
    t = jnp.exp2(-jnp.abs(f2))
    one_t = 1.0 + t
    b2 = (c2 + jnp.minimum(f2, 0.0)) - jnp.log(one_t) * LOG2_E
    e = jnp.exp2(-jnp.abs(a2 - b2))
    o_refs[0][...] = jnp.maximum(a2, b2) + jnp.log(1.0 + e) * LOG2_E
    o_refs[1][...] = ((1.0 - lb) * jnp.where(f >= 0.0, t, 1.0) / one_t).astype(o_refs[1].dtype)


def _split3(x):
    hi = x.astype(BF16)
    r = x - hi.astype(F32)
    mid = r.astype(BF16)
    lo = (r - mid.astype(F32)).astype(BF16)
    return hi, mid, lo


def _attn_tables(sinks, *, n_kv, group, blk):
    n_heads = n_kv * group
    q_feat = jnp.zeros((n_heads, blk, LANES), BF16)
    for i, s_i in enumerate(_split3(sinks)):
        q_feat = q_feat.at[:, :, i].set(jnp.broadcast_to(s_i[:, None], (n_heads, blk)))
    q_feat = q_feat.reshape(n_kv, group * blk, LANES)
    k_feat = np.zeros((2 * blk, LANES), np.float32)
    k_feat[0, 0:3] = 1.0
    qi = np.arange(blk)[:, None]
    kj = np.arange(2 * blk)[None, :]
    dist = (qi + blk - kj).astype(np.float32)
    valid = (dist >= 0) & (dist < WINDOW)
    sink_col = kj == 0
    inf = np.float32(np.inf)
    dist_all = np.where(sink_col, np.float32(0), np.where(valid, dist, inf))
    dist_first = np.where(sink_col, np.float32(0), np.where(valid & (kj >= blk), dist, inf))
    return q_feat, jnp.asarray(k_feat, BF16), jnp.asarray(dist_all), jnp.asarray(dist_first)


def _attn_kernel(slope_ref, qf_ref, kf_ref, dist_ref, dist_first_ref, q_ref, kc_ref, kp_ref, vc_ref,
                 vp_ref, o_ref, d_scr, *, n_kv, group):
    blk = q_ref.shape[0]
    d_scr[...] = jnp.where(pl.program_id(1) > 0, dist_ref[...], dist_first_ref[...])
    keep = (lax.broadcasted_iota(jnp.int32, (16, HEAD_DIM), 0) != 0).astype(BF16)
    ones = jnp.ones((2 * blk, HEAD_DIM), BF16)
    k_feat = kf_ref[...]

    def kv_head(hk, carry):
        ko = pl.ds(pl.multiple_of(hk * HEAD_DIM, HEAD_DIM), HEAD_DIM)
        kp = kp_ref[:, ko]
        vp = vp_ref[:, ko]
        vp = jnp.concatenate([vp[:16] * keep, vp[16:]], axis=0)
        kp = jnp.concatenate([kp[:16] * keep, kp[16:]], axis=0)
        k2 = jnp.concatenate([jnp.concatenate([kp, kc_ref[:, ko]], axis=0), k_feat], axis=1)
        v2 = jnp.concatenate([jnp.concatenate([vp, vc_ref[:, ko]], axis=0), ones], axis=1)
        cols = [pl.ds(pl.multiple_of((hk * group + g) * HEAD_DIM, HEAD_DIM), HEAD_DIM)
                for g in range(group)]
        q4 = jnp.concatenate([q_ref[:, cg] for cg in cols], axis=0)
        q4 = jnp.concatenate([q4, qf_ref[hk]], axis=1)
        s = _dot_nt(q4, k2)
        dist = d_scr[...]
        logits = jnp.concatenate(
            [s[g * blk:(g + 1) * blk] - slope_ref[hk * group + g] * dist for g in range(group)],
            axis=0)
        m = jnp.max(logits, axis=-1, keepdims=True)
        p = jnp.exp(logits - m).astype(BF16)
        o = _dot(p, v2)
        out = o[:, :HEAD_DIM] / o[:, HEAD_DIM:]
        for g, cg in enumerate(cols):
            o_ref[:, cg] = out[g * blk:(g + 1) * blk].astype(o_ref.dtype)
        return carry

    lax.fori_loop(0, n_kv, kv_head, 0, unroll=True)


def _attention(qkv, slopes, sinks, *, batch, seq, n_heads, n_kv):
    m = qkv.shape[0]
    blk = WINDOW
    group = n_heads // n_kv
    nb = seq // blk
    dq = n_heads * HEAD_DIM
    dkv = n_kv * HEAD_DIM
    k_col = dq // dkv
    v_col = k_col + 1
    q_feat, k_feat, dist, dist_first = _attn_tables(sinks, n_kv=n_kv, group=group, blk=blk)
    cur = lambda b, n: b * nb + n
    prev = lambda b, n: b * nb + jnp.maximum(n - 1, 0)
    const2 = lambda b, n: (0, 0)
    blk_bytes = (2 * _nbytes((blk, dq), BF16) + 4 * _nbytes((blk, dkv), BF16)
                 + _nbytes(q_feat.shape, BF16) + 3 * _nbytes(dist.shape, F32))
    return pl.pallas_call(
        functools.partial(_attn_kernel, n_kv=n_kv, group=group),
        grid=(batch, nb),
        in_specs=[
            pl.BlockSpec(memory_space=pltpu.SMEM),
            pl.BlockSpec(q_feat.shape, lambda b, n: (0, 0, 0)),
            pl.BlockSpec(k_feat.shape, const2),
            pl.BlockSpec(dist.shape, const2),
            pl.BlockSpec(dist.shape, const2),
            pl.BlockSpec((blk, dq), lambda b, n: (cur(b, n), 0)),
            pl.BlockSpec((blk, dkv), lambda b, n: (cur(b, n), k_col)),
            pl.BlockSpec((blk, dkv), lambda b, n: (prev(b, n), k_col)),
            pl.BlockSpec((blk, dkv), lambda b, n: (cur(b, n), v_col)),
            pl.BlockSpec((blk, dkv), lambda b, n: (prev(b, n), v_col)),
        ],
        out_specs=pl.BlockSpec((blk, dq), lambda b, n: (cur(b, n), 0)),
        out_shape=jax.ShapeDtypeStruct((m, dq), BF16),
        scratch_shapes=[pltpu.VMEM(dist.shape, F32)],
        compiler_params=pltpu.CompilerParams(
            dimension_semantics=("arbitrary", "arbitrary"),
            vmem_limit_bytes=_vmem_limit(blk_bytes)),
        name="swa_attention",
    )(slopes, q_feat, k_feat, dist, dist_first, qkv, qkv, qkv, qkv, qkv)


def _hgrn_tables(c):
    halves = [c >> (lv + 1) for lv in range(int(np.log2(c)))]
    r = np.arange(c)[:, None]
    j = np.arange(c)[None, :]

    def partial_sums(half):
        ref = (r // (2 * half)) * (2 * half) + half - 1
        return np.where(r > ref, (j > ref) & (j <= r), (j > r) & (j <= ref))

    masks = [((r // (2 * h)) == (j // (2 * h))) & ((r % (2 * h)) >= h) & ((j % (2 * h)) < h)
             for h in halves]
    masks.append(r == j)
    w = np.concatenate([j <= r, partial_sums(4), partial_sums(2)], axis=0).astype(np.float32)
    w3 = np.concatenate([w, w, w], axis=1)
    return jnp.asarray(w3, BF16), jnp.asarray(np.stack(masks).astype(np.float32), BF16), halves


def _hgrn_kernel(w3_ref, mask_ref, gain_ref, q_ref, k_ref, v_ref, lf_ref, gate_ref, o_ref,
                 s_ref, d_scr, *, c, halves):
    @pl.when(pl.program_id(2) == 0)
    def _():
        s_ref[...] = jnp.zeros_like(s_ref)

    n_chunks = q_ref.shape[0] // c
    zero_h = jnp.zeros((c, LANES), BF16)
    n_lv = len(halves)

    def block_diag(x0, x1):
        return jnp.concatenate([jnp.concatenate([x0, zero_h], axis=1),
                                jnp.concatenate([zero_h, x1], axis=1)], axis=0)

    def both(x):
        return x[:, :LANES], x[:, LANES:]

    def pair_mask(i):
        mk = mask_ref[i]
        return jnp.concatenate([mk, mk], axis=1)

    def chunk(ci, carry):
        rows = pl.ds(pl.multiple_of(ci * c, c), c)
        q = lambda: q_ref[rows, :]
        k = lambda: k_ref[rows, :]
        lf3 = jnp.concatenate(_split3(lf_ref[rows, :]), axis=0)
        d_scr[...] = _dot(w3_ref[...], lf3)
        b = lambda: d_scr[0:c, :]

        def level_exponent(half):
            pieces = []
            for g0 in range(0, c, 2 * half):
                ref = jnp.broadcast_to(d_scr[g0 + half - 1:g0 + half, :], (half, 2 * LANES))
                pieces += [ref - d_scr[g0:g0 + half, :], d_scr[g0 + half:g0 + 2 * half, :] - ref]
            return jnp.concatenate(pieces, axis=0)

        e1 = jnp.exp2(lf_ref[rows, :]).astype(BF16)
        p = _dot_nt(jnp.concatenate([q(), q() * e1], axis=0), block_diag(*both(k())))
        a = p[0:c].astype(BF16) * pair_mask(n_lv) + p[c:2 * c].astype(BF16) * pair_mask(n_lv - 1)
        for li, half in enumerate(halves[:-1]):
            if half >= 8:
                dn = level_exponent(half)
            else:
                dn = d_scr[c:2 * c, :] if half == 4 else d_scr[2 * c:3 * c, :]
            e = jnp.exp2(dn).astype(BF16)
            a = a + _dot_nt(q() * e, block_diag(*both(k() * e))).astype(BF16) * pair_mask(li)
        intra = _dot(a, block_diag(*both(v_ref[rows, :])))

        s0 = s_ref[0]
        s1 = s_ref[1]
        inter = _dot_nt(q() * jnp.exp2(b()).astype(BF16),
                        block_diag(s0.astype(BF16), s1.astype(BF16)))
        o = inter + intra

        b_last = d_scr[c - 1:c, :]
        u = _dot_tn(v_ref[rows, :], k() * jnp.exp2(b_last - b()).astype(BF16))
        decay = jnp.exp2(b_last)
        s_ref[0] = s0 * decay[:, :LANES] + u[:LANES, :LANES]
        s_ref[1] = s1 * decay[:, LANES:] + u[LANES:, LANES:]

        gate = gate_ref[rows, :].astype(F32)
        gain = gain_ref[...]
        for hh in range(2):
            lanes = slice(hh * LANES, (hh + 1) * LANES)
            oh = o[:, lanes]
            y = oh * lax.rsqrt(jnp.mean(oh * oh, axis=-1, keepdims=True) + EPS) * gain
            o_ref[rows, lanes] = (y * gate[:, lanes]).astype(o_ref.dtype)
        return carry

    lax.fori_loop(0, n_chunks, chunk, 0, unroll=min(n_chunks, 16))


def _hgrn_recurrence(q, k, v, log_f, gate, gain, *, batch, seq, tb=4096):
    m, d = q.shape
    c = HGRN_CHUNK
    w3, masks, halves = _hgrn_tables(c)
    tb = min(tb, seq)
    nt = seq // tb
    width = 2 * HEAD_DIM
    tile = pl.BlockSpec((tb, width), lambda b, h, t: (b * nt + t, h))
    const2 = lambda b, h, t: (0, 0)
    blk = (_nbytes((tb, width), F32) + 5 * _nbytes((tb, width), BF16)
           + _nbytes(w3.shape, BF16) + _nbytes(masks.shape, BF16))
    return pl.pallas_call(
        functools.partial(_hgrn_kernel, c=c, halves=halves),
        grid=(batch, d // width, nt),
        in_specs=[
            pl.BlockSpec(w3.shape, const2),
            pl.BlockSpec(masks.shape, lambda b, h, t: (0, 0, 0)),
            pl.BlockSpec((1, HEAD_DIM), const2),
            tile, tile, tile, tile, tile,
        ],
        out_specs=tile,
        out_shape=jax.ShapeDtypeStruct((m, d), BF16),
        scratch_shapes=[pltpu.VMEM((2, HEAD_DIM, HEAD_DIM), F32), pltpu.VMEM((3 * c, width), F32)],
        compiler_params=pltpu.CompilerParams(
            dimension_semantics=("arbitrary", "arbitrary", "arbitrary"),
            vmem_limit_bytes=_vmem_limit(blk)),
        name="hgrn2_recurrence",
    )(w3, masks, gain.reshape(1, HEAD_DIM), q, k, v, log_f, gate)


def _bcast_mod(mod_l, idx, batch):
    d = mod_l.shape[-1] // N_MOD
    return mod_l[:batch, idx * d:(idx + 1) * d].reshape(batch, 1, d)


def _gate_up(u, w_gu, casts):
    f = w_gu.shape[1] // 2
    tn = SWIGLU_TILE_N
    return _mm(u, w_gu, [0, f // tn], f, [], [BF16], _ep_swiglu, name="ffn_gate_up",
               casts=casts, tn=tn)


def _residual_mm(x, w, h, gate, coef, *, rows_per_batch, name):
    tm, tn = TILE_M, TILE_N
    tpb = rows_per_batch // tm
    tk = min(FULL_K, x.shape[1])
    for k0 in range(x.shape[1] // tk):
        extras = [
            (h, (tm, tn), lambda i, j, k: (i, j)),
            (gate, (1, 1, tn), lambda i, j, k: (i // tpb, 0, j)),
        ]
        h, = _mm(x, w, [0], w.shape[1], extras, [F32], functools.partial(_ep_residual, coef=coef),
                 name=name, k_slice=(k0, 1), tm=tm, tn=tn, tk=tk)
    return h


def kernel(x, c, ada_w, ada_b, norm_gains, ffn1_w_gu, ffn1_w_down, ffn2_w_gu, ffn2_w_down,
           attn_w_qkv, attn_w_o, attn_q_gain, attn_k_gain, attn_sinks,
           hgrn_w_qfig, hgrn_w_o, hgrn_g_gain, hgrn_lower_bounds):
    batch, seq, d = x.shape
    depth = ada_w.shape[0]
    m = batch * seq
    n_mixers = 2

    c_pad = jnp.zeros((8, d), F32).at[:batch].set(c)
    mod = _ada(c_pad, ada_w, ada_b)

    w_gu = _cast_bf16(ffn1_w_gu, 0)
    h = x.reshape(m, d)
    for i in range(depth):
        md = lambda idx: _bcast_mod(mod[i], idx, batch)
        jm = i // n_mixers
        is_attn = i % n_mixers == 0
        u = _norm_mod(h, norm_gains[i, 0], md(0), md(1), rows_per_batch=seq)
        mixer_w = (attn_w_qkv, attn_w_o) if is_attn else (hgrn_w_qfig, hgrn_w_o)
        act, w_down, w_mix, w_o, w_gu = _gate_up(
            u, w_gu, [(ffn1_w_down, i), (mixer_w[0], jm), (mixer_w[1], jm), (ffn2_w_gu, i)])
        h = _residual_mm(act, w_down, h, md(2), 0.5, rows_per_batch=seq, name="ffn_down")
        u = _norm_mod(h, norm_gains[i, 1], md(3), md(4), rows_per_batch=seq)
        if is_attn:
            n_heads = attn_sinks.shape[1]
            n_qkv = w_mix.shape[1]
            n_kv = (n_qkv // HEAD_DIM - n_heads) // 2
            tn = n_kv * HEAD_DIM
            q_tiles = n_heads // n_kv
            ones = jnp.ones((1, HEAD_DIM), F32)
            gains = jnp.stack([attn_q_gain[jm].reshape(1, HEAD_DIM) * HEAD_DIM ** -0.5] * q_tiles
                              + [attn_k_gain[jm].reshape(1, HEAD_DIM), ones])
            qkv, = _mm(u, w_mix, [0], n_qkv,
                       [(gains, (1, 1, HEAD_DIM), lambda i_, j_, k_: (j_, 0, 0))], [BF16],
                       functools.partial(_ep_qkv, n_norm_tiles=q_tiles + 1),
                       name="attn_qkv", tn=tn)
            heads = jnp.arange(1, n_heads + 1, dtype=F32)
            slopes = jnp.exp2(-8.0 * heads / n_heads)
            y = _attention(qkv, slopes, attn_sinks[jm].astype(F32), batch=batch, seq=seq,
                           n_heads=n_heads, n_kv=n_kv)
            mixer_name = "attn_out"
        else:
            fd = (w_mix.shape[1] - 2 * d) // 2
            tn = TILE_N
            qh, = _mm(u, w_mix, [0], fd, [], [BF16], _ep_silu, name="hgrn_q", tn=tn)
            lbp = hgrn_lower_bounds.astype(F32)
            log_f, kh = _mm(u, w_mix, [fd // tn], fd,
                            [(lbp, (depth, tn), lambda i_, j_, k_: (0, j_))], [F32, BF16],
                            functools.partial(_ep_forget, layer=i), name="hgrn_f", tn=tn)
            vh, = _mm(u, w_mix, [2 * fd // tn], d, [], [BF16], _ep_cast, name="hgrn_i", tn=tn)
            gh, = _mm(u, w_mix, [(2 * fd + d) // tn], d, [], [BF16], _ep_silu, name="hgrn_g", tn=tn)
            y = _hgrn_recurrence(qh, kh, vh, log_f, gh, hgrn_g_gain[jm], batch=batch, seq=seq)
            mixer_name = "hgrn_out"
        h = _residual_mm(y, w_o, h, md(5), 1.0, rows_per_batch=seq, name=mixer_name)
        u = _norm_mod(h, norm_gains[i, 2], md(6), md(7), rows_per_batch=seq)
        later = [(ffn2_w_down, i)] + ([(ffn1_w_gu, i + 1)] if i + 1 < depth else [])
        act, w_down, *next_gu = _gate_up(u, w_gu, later)
        h = _residual_mm(act, w_down, h, md(8), 0.5, rows_per_batch=seq, name="ffn_down")
        if next_gu:
            w_gu, = next_gu
    return h.reshape(batch, seq, d)
```

```python
import functools

import numpy as np
import jax
import jax.numpy as jnp
from jax import lax
from jax.experimental import pallas as pl
from jax.experimental.pallas import tpu as pltpu

F32 = jnp.float32
BF16 = jnp.bfloat16
EPS = 1e-6
LOG2_E = 1.4426950408889634

WINDOW = 128
HEAD_DIM = 128
N_MOD = 9
HGRN_CHUNK = 128

V7X_VMEM_BYTES = 64 * 1024 * 1024
VMEM_HEADROOM_BYTES = 6 * 1024 * 1024
LANES = 128

TILE_M = 1024
TILE_N = 1024
FULL_K = 4096
SWIGLU_TILE_N = TILE_N // 2


def _vmem_limit(block_bytes):
    want = 2 * block_bytes + VMEM_HEADROOM_BYTES
    return int(min(max(want, 32 * 1024 * 1024), V7X_VMEM_BYTES - 2 * 1024 * 1024))


def _nbytes(shape, dtype):
    return int(np.prod(shape)) * jnp.dtype(dtype).itemsize


def _silu(x):
    half = 0.5 * x
    return half + half * jnp.tanh(half)


def _dot(a, b):
    return jnp.dot(a, b, preferred_element_type=F32)


def _dot_nt(a, b):
    return lax.dot_general(a, b, (((1,), (1,)), ((), ())), preferred_element_type=F32)


def _dot_tn(a, b):
    return lax.dot_general(a, b, (((0,), (0,)), ((), ())), preferred_element_type=F32)


def _ada_kernel(c_ref, w_ref, b_ref, o_ref):
    cond = _silu(c_ref[...]).astype(BF16)
    o_ref[0] = _dot(cond, w_ref[0].astype(BF16)) + b_ref[0]


def _ada(c_pad, ada_w, ada_b, *, tn=512):
    depth, d, nd = ada_w.shape
    rows = c_pad.shape[0]
    blk = _nbytes((d, tn), F32) + _nbytes((rows, tn), F32) + _nbytes((rows, d), F32)
    return pl.pallas_call(
        _ada_kernel,
        grid=(depth, nd // tn),
        in_specs=[
            pl.BlockSpec((rows, d), lambda l, j: (0, 0)),
            pl.BlockSpec((1, d, tn), lambda l, j: (l, 0, j)),
            pl.BlockSpec((1, 1, tn), lambda l, j: (l, 0, j)),
        ],
        out_specs=pl.BlockSpec((1, rows, tn), lambda l, j: (l, 0, j)),
        out_shape=jax.ShapeDtypeStruct((depth, rows, nd), F32),
        compiler_params=pltpu.CompilerParams(
            dimension_semantics=("arbitrary", "arbitrary"),
            vmem_limit_bytes=_vmem_limit(blk)),
        name="ada_mod",
    )(c_pad, ada_w, ada_b.reshape(depth, 1, nd))


def _norm_mod_kernel(h_ref, gain_ref, shift_ref, scale_ref, o_ref):
    h = h_ref[...]
    y = h * lax.rsqrt(jnp.mean(h * h, axis=-1, keepdims=True) + EPS) * gain_ref[...]
    o_ref[...] = (y * (1.0 + scale_ref[0]) + shift_ref[0]).astype(o_ref.dtype)


def _norm_mod(h, gain, shift, scale, *, rows_per_batch, tm=512):
    m, d = h.shape
    tpb = rows_per_batch // tm
    blk = _nbytes((tm, d), F32) + _nbytes((tm, d), BF16)
    return pl.pallas_call(
        _norm_mod_kernel,
        grid=(m // tm,),
        in_specs=[
            pl.BlockSpec((tm, d), lambda i: (i, 0)),
            pl.BlockSpec((1, d), lambda i: (0, 0)),
            pl.BlockSpec((1, 1, d), lambda i: (i // tpb, 0, 0)),
            pl.BlockSpec((1, 1, d), lambda i: (i // tpb, 0, 0)),
        ],
        out_specs=pl.BlockSpec((tm, d), lambda i: (i, 0)),
        out_shape=jax.ShapeDtypeStruct((m, d), BF16),
        compiler_params=pltpu.CompilerParams(
            dimension_semantics=("arbitrary",),
            vmem_limit_bytes=_vmem_limit(blk)),
        name="norm_mod",
    )(h, gain.reshape(1, d), shift, scale)


def _mm_kernel(*refs, n_w, n_e, n_o, nk, n_cast, epilogue):
    x_ref = refs[0]
    w_refs = refs[1:1 + n_w]
    e_refs = refs[1 + n_w:1 + n_w + n_e]
    n_in = 1 + n_w + n_e + n_cast
    o_refs = refs[n_in:n_in + n_o]
    acc_refs = refs[n_in + n_o + n_cast:]
    j = pl.program_id(1)

    for src_ref, dst_ref in zip(refs[n_in - n_cast:n_in], refs[n_in + n_o:n_in + n_o + n_cast]):
        dst_ref[...] = src_ref[...].astype(dst_ref.dtype)

    def dots():
        x = x_ref[...]
        return [_dot(x, w[...]) for w in w_refs]

    if nk == 1:
        epilogue(dots(), j, e_refs, o_refs)
        return

    k = pl.program_id(2)

    @pl.when(k == 0)
    def _():
        for a, d in zip(acc_refs, dots()):
            a[...] = d

    if nk > 2:
        @pl.when((k > 0) & (k < nk - 1))
        def _():
            for a, d in zip(acc_refs, dots()):
                a[...] += d

    @pl.when(k == nk - 1)
    def _():
        epilogue([a[...] + d for a, d in zip(acc_refs, dots())], j, e_refs, o_refs)


def _cast_chunk_rows(n_rows, n_steps):
    rows = 16
    while n_rows % rows or n_rows // rows > n_steps:
        rows += 16
    return rows


def _mm(x, w, w_col_blocks, n_cols, extras, outs, epilogue, *, name, casts=(), k_slice=None,
        tm=TILE_M, tn=TILE_N, tk=FULL_K):
    m, kdim = x.shape
    tk = min(tk, kdim)
    k0, nk = k_slice if k_slice is not None else (0, kdim // tk)
    grid = (m // tm, n_cols // tn, nk)
    in_specs = [pl.BlockSpec((tm, tk), lambda i, j, k: (i, k + k0))]
    blk = _nbytes((tm, tk), x.dtype)
    for off in w_col_blocks:
        in_specs.append(pl.BlockSpec((tk, tn), lambda i, j, k, off=off: (k + k0, j + off)))
        blk += _nbytes((tk, tn), w.dtype)
    for arr, bshape, imap in extras:
        in_specs.append(pl.BlockSpec(bshape, imap))
        blk += _nbytes(bshape, arr.dtype)
    out_specs = [pl.BlockSpec((tm, tn), lambda i, j, k: (i, j)) for _ in outs]
    out_shape = [jax.ShapeDtypeStruct((m, n_cols), dt) for dt in outs]
    blk += sum(_nbytes((tm, tn), dt) for dt in outs)
    operands = [x, *([w] * len(w_col_blocks)), *[e[0] for e in extras]]
    for src, layer in casts:
        _, n_rows, n_src_cols = src.shape
        rows = _cast_chunk_rows(n_rows, grid[0] * grid[1] * grid[2])
        chunk = lambda i, j, k, last=n_rows // rows - 1: jnp.minimum(
            (i * grid[1] + j) * grid[2] + k, last)
        in_specs.append(pl.BlockSpec((None, rows, n_src_cols),
                                     lambda i, j, k, layer=layer, chunk=chunk: (layer, chunk(i, j, k), 0)))
        out_specs.append(pl.BlockSpec((rows, n_src_cols), lambda i, j, k, chunk=chunk: (chunk(i, j, k), 0)))
        out_shape.append(jax.ShapeDtypeStruct((n_rows, n_src_cols), BF16))
        operands.append(src)
        blk += _nbytes((rows, n_src_cols), F32) + _nbytes((rows, n_src_cols), BF16)
    n_w = len(w_col_blocks)
    scratch = [pltpu.VMEM((tm, tn), F32) for _ in range(n_w)] if nk > 1 else []
    scratch_bytes = (len(scratch) + 3 * n_w) * _nbytes((tm, tn), F32)
    kern = functools.partial(_mm_kernel, n_w=n_w, n_e=len(extras), n_o=len(outs), nk=nk,
                             n_cast=len(casts), epilogue=epilogue)
    return pl.pallas_call(
        kern,
        grid=grid,
        in_specs=in_specs,
        out_specs=out_specs,
        out_shape=out_shape,
        scratch_shapes=scratch,
        compiler_params=pltpu.CompilerParams(
            dimension_semantics=("arbitrary", "arbitrary", "arbitrary"),
            vmem_limit_bytes=_vmem_limit(blk + scratch_bytes // 2)),
        name=name,
    )(*operands)


def _cast_kernel(src_ref, o_ref):
    o_ref[...] = src_ref[...].astype(o_ref.dtype)


def _cast_bf16(src, layer, *, rows=128):
    _, n_rows, n_cols = src.shape
    blk = _nbytes((rows, n_cols), F32) + _nbytes((rows, n_cols), BF16)
    return pl.pallas_call(
        _cast_kernel,
        grid=(n_rows // rows,),
        in_specs=[pl.BlockSpec((None, rows, n_cols), lambda r: (layer, r, 0))],
        out_specs=pl.BlockSpec((rows, n_cols), lambda r: (r, 0)),
        out_shape=jax.ShapeDtypeStruct((n_rows, n_cols), BF16),
        compiler_params=pltpu.CompilerParams(
            dimension_semantics=("arbitrary",),
            vmem_limit_bytes=_vmem_limit(blk)),
        name="cast_weights",
    )(src)


def _ep_swiglu(accs, j, e_refs, o_refs):
    a, b = accs
    o_refs[0][...] = (_silu(a) * b).astype(o_refs[0].dtype)


def _ep_residual(accs, j, e_refs, o_refs, *, coef):
    h_ref, gate_ref = e_refs
    o_refs[0][...] = h_ref[...] + (coef * gate_ref[0]) * accs[0]


def _ep_qkv(accs, j, e_refs, o_refs, *, n_norm_tiles):
    y = accs[0]
    gain_ref, = e_refs
    o_ref = o_refs[0]
    tn = y.shape[1]

    @pl.when(j < n_norm_tiles)
    def _():
        g = gain_ref[0]
        for hd in range(tn // HEAD_DIM):
            yh = y[:, hd * HEAD_DIM:(hd + 1) * HEAD_DIM]
            inv = lax.rsqrt(jnp.mean(yh * yh, axis=-1, keepdims=True) + EPS)
            o_ref[:, hd * HEAD_DIM:(hd + 1) * HEAD_DIM] = (yh * inv * g).astype(o_ref.dtype)

    @pl.when(j >= n_norm_tiles)
    def _():
        o_ref[...] = y.astype(o_ref.dtype)


def _ep_silu(accs, j, e_refs, o_refs):
    o_refs[0][...] = _silu(accs[0]).astype(o_refs[0].dtype)


def _ep_cast(accs, j, e_refs, o_refs):
    o_refs[0][...] = accs[0].astype(o_refs[0].dtype)


def _forget_constants(lbp, layer):
    depth = lbp.shape[0]
    mx = lbp[0:1]
    for l in range(1, depth):
        mx = jnp.maximum(mx, lbp[l:l + 1])
    es = [jnp.exp(lbp[l:l + 1] - mx) for l in range(depth)]
    tot = es[0]
    for l in range(1, depth):
        tot = tot + es[l]
    s = [e / tot for e in es]
    cum = s[0]
    for l in range(1, layer + 1):
        cum = cum + s[l]
    lb = cum - s[0]
    return jnp.log(lb) * LOG2_E, jnp.log1p(-lb) * LOG2_E, 1.0 - lb


def _forget_gate(f, a2, c2, one_m_lb):
    f2 = f * LOG2_E
    t = jnp.exp2(-jnp.abs(f2))
    one_t = 1.0 + t
    b2 = (c2 + jnp.minimum(f2, 0.0)) - jnp.log(one_t) * LOG2_E
    e = jnp.exp2(-jnp.abs(a2 - b2))
    log2_f = jnp.maximum(a2, b2) + jnp.log(1.0 + e) * LOG2_E
    return log2_f, one_m_lb * jnp.where(f >= 0.0, t, 1.0) / one_t


def _split3(x):
    hi = x.astype(BF16)
    r = x - hi.astype(F32)
    mid = r.astype(BF16)
    lo = (r - mid.astype(F32)).astype(BF16)
    return hi, mid, lo


def _attn_tables(sinks, *, n_kv, group, blk):
    n_heads = n_kv * group
    q_feat = jnp.zeros((n_heads, blk, LANES), BF16)
    for i, s_i in enumerate(_split3(sinks)):
        q_feat = q_feat.at[:, :, i].set(jnp.broadcast_to(s_i[:, None], (n_heads, blk)))
    q_feat = q_feat.reshape(n_kv, group * blk, LANES)
    k_feat = np.zeros((2 * blk, LANES), np.float32)
    k_feat[0, 0:3] = 1.0
    qi = np.arange(blk)[:, None]
    kj = np.arange(2 * blk)[None, :]
    dist = (qi + blk - kj).astype(np.float32)
    valid = (dist >= 0) & (dist < WINDOW)
    sink_col = kj == 0
    inf = np.float32(np.inf)
    dist_all = np.where(sink_col, np.float32(0), np.where(valid, dist, inf))
    dist_first = np.where(sink_col, np.float32(0), np.where(valid & (kj >= blk), dist, inf))
    return q_feat, jnp.asarray(k_feat, BF16), jnp.asarray(dist_all), jnp.asarray(dist_first)


def _attn_kernel(slope_ref, qf_ref, kf_ref, dist_ref, dist_first_ref, q_ref, kc_ref, kp_ref, vc_ref,
                 vp_ref, o_ref, d_scr, *, n_kv, group):
    blk = q_ref.shape[0]
    d_scr[...] = jnp.where(pl.program_id(1) > 0, dist_ref[...], dist_first_ref[...])
    keep = (lax.broadcasted_iota(jnp.int32, (16, HEAD_DIM), 0) != 0).astype(BF16)
    ones = jnp.ones((2 * blk, HEAD_DIM), BF16)
    k_feat = kf_ref[...]

    def kv_head(hk, carry):
        ko = pl.ds(pl.multiple_of(hk * HEAD_DIM, HEAD_DIM), HEAD_DIM)
        kp = kp_ref[:, ko]
        vp = vp_ref[:, ko]
        vp = jnp.concatenate([vp[:16] * keep, vp[16:]], axis=0)
        kp = jnp.concatenate([kp[:16] * keep, kp[16:]], axis=0)
        k2 = jnp.concatenate([jnp.concatenate([kp, kc_ref[:, ko]], axis=0), k_feat], axis=1)
        v2 = jnp.concatenate([jnp.concatenate([vp, vc_ref[:, ko]], axis=0), ones], axis=1)
        cols = [pl.ds(pl.multiple_of((hk * group + g) * HEAD_DIM, HEAD_DIM), HEAD_DIM)
                for g in range(group)]
        q4 = jnp.concatenate([q_ref[:, cg] for cg in cols], axis=0)
        q4 = jnp.concatenate([q4, qf_ref[hk]], axis=1)
        s = _dot_nt(q4, k2)
        dist = d_scr[...]
        logits = jnp.concatenate(
            [s[g * blk:(g + 1) * blk] - slope_ref[hk * group + g] * dist for g in range(group)],
            axis=0)
        m = jnp.max(logits, axis=-1, keepdims=True)
        p = jnp.exp(logits - m).astype(BF16)
        o = _dot(p, v2)
        out = o[:, :HEAD_DIM] / o[:, HEAD_DIM:]
        for g, cg in enumerate(cols):
            o_ref[:, cg] = out[g * blk:(g + 1) * blk].astype(o_ref.dtype)
        return carry

    lax.fori_loop(0, n_kv, kv_head, 0, unroll=True)


def _attention(qkv, slopes, sinks, *, batch, seq, n_heads, n_kv):
    m = qkv.shape[0]
    blk = WINDOW
    group = n_heads // n_kv
    nb = seq // blk
    dq = n_heads * HEAD_DIM
    dkv = n_kv * HEAD_DIM
    k_col = dq // dkv
    v_col = k_col + 1
    q_feat, k_feat, dist, dist_first = _attn_tables(sinks, n_kv=n_kv, group=group, blk=blk)
    cur = lambda b, n: b * nb + n
    prev = lambda b, n: b * nb + jnp.maximum(n - 1, 0)
    const2 = lambda b, n: (0, 0)
    blk_bytes = (2 * _nbytes((blk, dq), BF16) + 4 * _nbytes((blk, dkv), BF16)
                 + _nbytes(q_feat.shape, BF16) + 3 * _nbytes(dist.shape, F32))
    return pl.pallas_call(
        functools.partial(_attn_kernel, n_kv=n_kv, group=group),
        grid=(batch, nb),
        in_specs=[
            pl.BlockSpec(memory_space=pltpu.SMEM),
            pl.BlockSpec(q_feat.shape, lambda b, n: (0, 0, 0)),
            pl.BlockSpec(k_feat.shape, const2),
            pl.BlockSpec(dist.shape, const2),
            pl.BlockSpec(dist.shape, const2),
            pl.BlockSpec((blk, dq), lambda b, n: (cur(b, n), 0)),
            pl.BlockSpec((blk, dkv), lambda b, n: (cur(b, n), k_col)),
            pl.BlockSpec((blk, dkv), lambda b, n: (prev(b, n), k_col)),
            pl.BlockSpec((blk, dkv), lambda b, n: (cur(b, n), v_col)),
            pl.BlockSpec((blk, dkv), lambda b, n: (prev(b, n), v_col)),
        ],
        out_specs=pl.BlockSpec((blk, dq), lambda b, n: (cur(b, n), 0)),
        out_shape=jax.ShapeDtypeStruct((m, dq), BF16),
        scratch_shapes=[pltpu.VMEM(dist.shape, F32)],
        compiler_params=pltpu.CompilerParams(
            dimension_semantics=("arbitrary", "arbitrary"),
            vmem_limit_bytes=_vmem_limit(blk_bytes)),
        name="swa_attention",
    )(slopes, q_feat, k_feat, dist, dist_first, qkv, qkv, qkv, qkv, qkv)


def _hgrn_tables(c):
    halves = [c >> (lv + 1) for lv in range(int(np.log2(c)))]
    r = np.arange(c)[:, None]
    j = np.arange(c)[None, :]

    def partial_sums(half):
        ref = (r // (2 * half)) * (2 * half) + half - 1
        return np.where(r > ref, (j > ref) & (j <= r), (j > r) & (j <= ref))

    masks = [((r // (2 * h)) == (j // (2 * h))) & ((r % (2 * h)) >= h) & ((j % (2 * h)) < h)
             for h in halves]
    masks.append(r == j)
    w = np.concatenate([j <= r, partial_sums(4), partial_sums(2)], axis=0).astype(np.float32)
    w3 = np.concatenate([w, w, w], axis=1)
    return jnp.asarray(w3, BF16), jnp.asarray(np.stack(masks).astype(np.float32), BF16), halves


def _hgrn_kernel(w3_ref, mask_ref, gain_ref, lbp_ref, q_ref, f_ref, v_ref, gate_ref, o_ref,
                 s_ref, d_scr, lf_scr, k_scr, *, c, halves, layer):
    @pl.when(pl.program_id(2) == 0)
    def _():
        s_ref[...] = jnp.zeros_like(s_ref)

    n_chunks = q_ref.shape[0] // c
    zero_h = jnp.zeros((c, LANES), BF16)
    n_lv = len(halves)
    gate_consts = _forget_constants(lbp_ref[...], layer)

    def block_diag(x0, x1):
        return jnp.concatenate([jnp.concatenate([x0, zero_h], axis=1),
                                jnp.concatenate([zero_h, x1], axis=1)], axis=0)

    def both(x):
        return x[:, :LANES], x[:, LANES:]

    def pair_mask(i):
        mk = mask_ref[i]
        return jnp.concatenate([mk, mk], axis=1)

    def chunk(ci, carry):
        rows = pl.ds(pl.multiple_of(ci * c, c), c)
        q = lambda: q_ref[rows, :]
        log2_f, key = _forget_gate(f_ref[rows, :], *gate_consts)
        lf_scr[rows, :] = log2_f
        k_scr[rows, :] = key.astype(k_scr.dtype)
        k = lambda: k_scr[rows, :]
        lf3 = jnp.concatenate(_split3(lf_scr[rows, :]), axis=0)
        d_scr[...] = _dot(w3_ref[...], lf3)
        b = lambda: d_scr[0:c, :]

        def level_exponent(half):
            pieces = []
            for g0 in range(0, c, 2 * half):
                ref = jnp.broadcast_to(d_scr[g0 + half - 1:g0 + half, :], (half, 2 * LANES))
                pieces += [ref - d_scr[g0:g0 + half, :], d_scr[g0 + half:g0 + 2 * half, :] - ref]
            return jnp.concatenate(pieces, axis=0)

        e1 = jnp.exp2(lf_scr[rows, :]).astype(BF16)
        p = _dot_nt(jnp.concatenate([q(), q() * e1], axis=0), block_diag(*both(k())))
        a = p[0:c].astype(BF16) * pair_mask(n_lv) + p[c:2 * c].astype(BF16) * pair_mask(n_lv - 1)
        for li, half in enumerate(halves[:-1]):
            if half >= 8:
                dn = level_exponent(half)
            else:
                dn = d_scr[c:2 * c, :] if half == 4 else d_scr[2 * c:3 * c, :]
            e = jnp.exp2(dn).astype(BF16)
            a = a + _dot_nt(q() * e, block_diag(*both(k() * e))).astype(BF16) * pair_mask(li)
        intra = _dot(a, block_diag(*both(v_ref[rows, :])))

        s0 = s_ref[0]
        s1 = s_ref[1]
        inter = _dot_nt(q() * jnp.exp2(b()).astype(BF16),
                        block_diag(s0.astype(BF16), s1.astype(BF16)))
        o = inter + intra

        b_last = d_scr[c - 1:c, :]
        u = _dot_tn(v_ref[rows, :], k() * jnp.exp2(b_last - b()).astype(BF16))
        decay = jnp.exp2(b_last)
        s_ref[0] = s0 * decay[:, :LANES] + u[:LANES, :LANES]
        s_ref[1] = s1 * decay[:, LANES:] + u[LANES:, LANES:]

        gate = gate_ref[rows, :].astype(F32)
        gain = gain_ref[...]
        for hh in range(2):
            lanes = slice(hh * LANES, (hh + 1) * LANES)
            oh = o[:, lanes]
            y = oh * lax.rsqrt(jnp.mean(oh * oh, axis=-1, keepdims=True) + EPS) * gain
            o_ref[rows, lanes] = (y * gate[:, lanes]).astype(o_ref.dtype)
        return carry

    lax.fori_loop(0, n_chunks, chunk, 0, unroll=min(n_chunks, 16))


def _hgrn_recurrence(q, f, v, gate, gain, lower_bounds, *, layer, batch, seq, tb=4096):
    m, d = q.shape
    c = HGRN_CHUNK
    w3, masks, halves = _hgrn_tables(c)
    tb = min(tb, seq)
    nt = seq // tb
    width = 2 * HEAD_DIM
    tile = pl.BlockSpec((tb, width), lambda b, h, t: (b * nt + t, h))
    const2 = lambda b, h, t: (0, 0)
    depth = lower_bounds.shape[0]
    blk = (_nbytes((tb, width), F32) + 4 * _nbytes((tb, width), BF16)
           + _nbytes(w3.shape, BF16) + _nbytes(masks.shape, BF16)
           + (_nbytes((tb, width), F32) + _nbytes((tb, width), BF16)) // 2)
    return pl.pallas_call(
        functools.partial(_hgrn_kernel, c=c, halves=halves, layer=layer),
        grid=(batch, d // width, nt),
        in_specs=[
            pl.BlockSpec(w3.shape, const2),
            pl.BlockSpec(masks.shape, lambda b, h, t: (0, 0, 0)),
            pl.BlockSpec((1, HEAD_DIM), const2),
            pl.BlockSpec((depth, width), lambda b, h, t: (0, h)),
            tile, tile, tile, tile,
        ],
        out_specs=tile,
        out_shape=jax.ShapeDtypeStruct((m, d), BF16),
        scratch_shapes=[pltpu.VMEM((2, HEAD_DIM, HEAD_DIM), F32), pltpu.VMEM((3 * c, width), F32),
                        pltpu.VMEM((tb, width), F32), pltpu.VMEM((tb, width), BF16)],
        compiler_params=pltpu.CompilerParams(
            dimension_semantics=("arbitrary", "arbitrary", "arbitrary"),
            vmem_limit_bytes=_vmem_limit(blk)),
        name="hgrn2_recurrence",
    )(w3, masks, gain.reshape(1, HEAD_DIM), lower_bounds, q, f, v, gate)


def _bcast_mod(mod_l, idx, batch):
    d = mod_l.shape[-1] // N_MOD
    return mod_l[:batch, idx * d:(idx + 1) * d].reshape(batch, 1, d)


def _gate_up(u, w_gu, casts):
    f = w_gu.shape[1] // 2
    tn = SWIGLU_TILE_N
    return _mm(u, w_gu, [0, f // tn], f, [], [BF16], _ep_swiglu, name="ffn_gate_up",
               casts=casts, tn=tn)


def _residual_mm(x, w, h, gate, coef, *, rows_per_batch, name):
    tm, tn = TILE_M, TILE_N
    tpb = rows_per_batch // tm
    tk = min(FULL_K, x.shape[1])
    for k0 in range(x.shape[1] // tk):
        extras = [
            (h, (tm, tn), lambda i, j, k: (i, j)),
            (gate, (1, 1, tn), lambda i, j, k: (i // tpb, 0, j)),
        ]
        h, = _mm(x, w, [0], w.shape[1], extras, [F32], functools.partial(_ep_residual, coef=coef),
                 name=name, k_slice=(k0, 1), tm=tm, tn=tn, tk=tk)
    return h


def kernel(x, c, ada_w, ada_b, norm_gains, ffn1_w_gu, ffn1_w_down, ffn2_w_gu, ffn2_w_down,
           attn_w_qkv, attn_w_o, attn_q_gain, attn_k_gain, attn_sinks,
           hgrn_w_qfig, hgrn_w_o, hgrn_g_gain, hgrn_lower_bounds):
    batch, seq, d = x.shape
    depth = ada_w.shape[0]
    m = batch * seq
    n_mixers = 2

    c_pad = jnp.zeros((8, d), F32).at[:batch].set(c)
    mod = _ada(c_pad, ada_w, ada_b)

    w_gu = _cast_bf16(ffn1_w_gu, 0)
    h = x.reshape(m, d)
    for i in range(depth):
        md = lambda idx: _bcast_mod(mod[i], idx, batch)
        jm = i // n_mixers
        is_attn = i % n_mixers == 0
        u = _norm_mod(h, norm_gains[i, 0], md(0), md(1), rows_per_batch=seq)
        mixer_w = (attn_w_qkv, attn_w_o) if is_attn else (hgrn_w_qfig, hgrn_w_o)
        act, w_down, w_mix, w_o, w_gu = _gate_up(
            u, w_gu, [(ffn1_w_down, i), (mixer_w[0], jm), (mixer_w[1], jm), (ffn2_w_gu, i)])
        h = _residual_mm(act, w_down, h, md(2), 0.5, rows_per_batch=seq, name="ffn_down")
        u = _norm_mod(h, norm_gains[i, 1], md(3), md(4), rows_per_batch=seq)
        if is_attn:
            n_heads = attn_sinks.shape[1]
            n_qkv = w_mix.shape[1]
            n_kv = (n_qkv // HEAD_DIM - n_heads) // 2
            tn = n_kv * HEAD_DIM
            q_tiles = n_heads // n_kv
            ones = jnp.ones((1, HEAD_DIM), F32)
            gains = jnp.stack([attn_q_gain[jm].reshape(1, HEAD_DIM) * HEAD_DIM ** -0.5] * q_tiles
                              + [attn_k_gain[jm].reshape(1, HEAD_DIM), ones])
            qkv, = _mm(u, w_mix, [0], n_qkv,
                       [(gains, (1, 1, HEAD_DIM), lambda i_, j_, k_: (j_, 0, 0))], [BF16],
                       functools.partial(_ep_qkv, n_norm_tiles=q_tiles + 1),
                       name="attn_qkv", tn=tn)
            heads = jnp.arange(1, n_heads + 1, dtype=F32)
            slopes = jnp.exp2(-8.0 * heads / n_heads)
            y = _attention(qkv, slopes, attn_sinks[jm].astype(F32), batch=batch, seq=seq,
                           n_heads=n_heads, n_kv=n_kv)
            mixer_name = "attn_out"
        else:
            fd = (w_mix.shape[1] - 2 * d) // 2
            tn = TILE_N
            qh, = _mm(u, w_mix, [0], fd, [], [BF16], _ep_silu, name="hgrn_q", tn=tn)
            fh, = _mm(u, w_mix, [fd // tn], fd, [], [F32], _ep_cast, name="hgrn_f", tn=tn)
            vh, = _mm(u, w_mix, [2 * fd // tn], d, [], [BF16], _ep_cast, name="hgrn_i", tn=tn)
            gh, = _mm(u, w_mix, [(2 * fd + d) // tn], d, [], [BF16], _ep_silu, name="hgrn_g", tn=tn)
            y = _hgrn_recurrence(qh, fh, vh, gh, hgrn_g_gain[jm], hgrn_lower_bounds.astype(F32),
                                 layer=i, batch=batch, seq=seq)
            mixer_name = "hgrn_out"
        h = _residual_mm(y, w_o, h, md(5), 1.0, rows_per_batch=seq, name=mixer_name)
        u = _norm_mod(h, norm_gains[i, 2], md(6), md(7), rows_per_batch=seq)
        later = [(ffn2_w_down, i)] + ([(ffn1_w_gu, i + 1)] if i + 1 < depth else [])
        act, w_down, *next_gu = _gate_up(u, w_gu, later)
        h = _residual_mm(act, w_down, h, md(8), 0.5, rows_per_batch=seq, name="ffn_down")
        if next_gu:
            w_gu, = next_gu
    return h.reshape(batch, seq, d)
```

```python
import functools

import numpy as np
import jax
import jax.numpy as jnp
from jax import lax
from jax.experimental import pallas as pl
from jax.experimental.pallas import tpu as pltpu

F32 = jnp.float32
BF16 = jnp.bfloat16
EPS = 1e-6
LOG2_E = 1.4426950408889634

WINDOW = 128
HEAD_DIM = 128
N_MOD = 9
HGRN_CHUNK = 128

V7X_VMEM_BYTES = 64 * 1024 * 1024
VMEM_HEADROOM_BYTES = 6 * 1024 * 1024
LANES = 128

TILE_M = 1024
TILE_N = 1024
FULL_K = 4096
SWIGLU_TILE_N = TILE_N // 2


def _vmem_limit(block_bytes):
    want = 2 * block_bytes + VMEM_HEADROOM_BYTES
    return int(min(max(want, 32 * 1024 * 1024), V7X_VMEM_BYTES - 2 * 1024 * 1024))


def _nbytes(shape, dtype):
    return int(np.prod(shape)) * jnp.dtype(dtype).itemsize


def _silu(x):
    half = 0.5 * x
    return half + half * jnp.tanh(half)


def _dot(a, b):
    return jnp.dot(a, b, preferred_element_type=F32)


def _dot_nt(a, b):
    return lax.dot_general(a, b, (((1,), (1,)), ((), ())), preferred_element_type=F32)


def _dot_tn(a, b):
    return lax.dot_general(a, b, (((0,), (0,)), ((), ())), preferred_element_type=F32)


def _ada_kernel(c_ref, w_ref, b_ref, o_ref):
    cond = _silu(c_ref[...]).astype(BF16)
    o_ref[0] = _dot(cond, w_ref[0].astype(BF16)) + b_ref[0]


def _ada(c_pad, ada_w, ada_b, *, tn=512):
    depth, d, nd = ada_w.shape
    rows = c_pad.shape[0]
    blk = _nbytes((d, tn), F32) + _nbytes((rows, tn), F32) + _nbytes((rows, d), F32)
    return pl.pallas_call(
        _ada_kernel,
        grid=(depth, nd // tn),
        in_specs=[
            pl.BlockSpec((rows, d), lambda l, j: (0, 0)),
            pl.BlockSpec((1, d, tn), lambda l, j: (l, 0, j)),
            pl.BlockSpec((1, 1, tn), lambda l, j: (l, 0, j)),
        ],
        out_specs=pl.BlockSpec((1, rows, tn), lambda l, j: (l, 0, j)),
        out_shape=jax.ShapeDtypeStruct((depth, rows, nd), F32),
        compiler_params=pltpu.CompilerParams(
            dimension_semantics=("arbitrary", "arbitrary"),
            vmem_limit_bytes=_vmem_limit(blk)),
        name="ada_mod",
    )(c_pad, ada_w, ada_b.reshape(depth, 1, nd))


def _norm_mod_kernel(h_ref, gain_ref, shift_ref, scale_ref, o_ref):
    h = h_ref[...]
    y = h * lax.rsqrt(jnp.mean(h * h, axis=-1, keepdims=True) + EPS) * gain_ref[...]
    o_ref[...] = (y * (1.0 + scale_ref[0]) + shift_ref[0]).astype(o_ref.dtype)


def _norm_mod(h, gain, shift, scale, *, rows_per_batch, tm=512):
    m, d = h.shape
    tpb = rows_per_batch // tm
    blk = _nbytes((tm, d), F32) + _nbytes((tm, d), BF16)
    return pl.pallas_call(
        _norm_mod_kernel,
        grid=(m // tm,),
        in_specs=[
            pl.BlockSpec((tm, d), lambda i: (i, 0)),
            pl.BlockSpec((1, d), lambda i: (0, 0)),
            pl.BlockSpec((1, 1, d), lambda i: (i // tpb, 0, 0)),
            pl.BlockSpec((1, 1, d), lambda i: (i // tpb, 0, 0)),
        ],
        out_specs=pl.BlockSpec((tm, d), lambda i: (i, 0)),
        out_shape=jax.ShapeDtypeStruct((m, d), BF16),
        compiler_params=pltpu.CompilerParams(
            dimension_semantics=("arbitrary",),
            vmem_limit_bytes=_vmem_limit(blk)),
        name="norm_mod",
    )(h, gain.reshape(1, d), shift, scale)


def _mm_kernel(*refs, n_w, n_e, n_o, nk, n_cast, epilogue):
    x_ref = refs[0]
    w_refs = refs[1:1 + n_w]
    e_refs = refs[1 + n_w:1 + n_w + n_e]
    n_in = 1 + n_w + n_e + n_cast
    o_refs = refs[n_in:n_in + n_o]
    acc_refs = refs[n_in + n_o + n_cast:]
    j = pl.program_id(1)

    for src_ref, dst_ref in zip(refs[n_in - n_cast:n_in], refs[n_in + n_o:n_in + n_o + n_cast]):
        dst_ref[...] = src_ref[...].astype(dst_ref.dtype)

    def dots():
        x = x_ref[...]
        return [_dot(x, w[...]) for w in w_refs]

    if nk == 1:
        epilogue(dots(), j, e_refs, o_refs)
        return

    k = pl.program_id(2)

    @pl.when(k == 0)
    def _():
        for a, d in zip(acc_refs, dots()):
            a[...] = d

    if nk > 2:
        @pl.when((k > 0) & (k < nk - 1))
        def _():
            for a, d in zip(acc_refs, dots()):
                a[...] += d

    @pl.when(k == nk - 1)
    def _():
        epilogue([a[...] + d for a, d in zip(acc_refs, dots())], j, e_refs, o_refs)


def _cast_chunk_rows(n_rows, n_steps):
    rows = 16
    while n_rows % rows or n_rows // rows > n_steps:
        rows += 16
    return rows


def _mm(x, w, w_col_blocks, n_cols, extras, outs, epilogue, *, name, casts=(), k_slice=None,
        tm=TILE_M, tn=TILE_N, tk=FULL_K):
    m, kdim = x.shape
    tk = min(tk, kdim)
    k0, nk = k_slice if k_slice is not None else (0, kdim // tk)
    grid = (m // tm, n_cols // tn, nk)
    in_specs = [pl.BlockSpec((tm, tk), lambda i, j, k: (i, k + k0))]
    blk = _nbytes((tm, tk), x.dtype)
    for off in w_col_blocks:
        in_specs.append(pl.BlockSpec((tk, tn), lambda i, j, k, off=off: (k + k0, j + off)))
        blk += _nbytes((tk, tn), w.dtype)
    for arr, bshape, imap in extras:
        in_specs.append(pl.BlockSpec(bshape, imap))
        blk += _nbytes(bshape, arr.dtype)
    out_specs = [pl.BlockSpec((tm, tn), lambda i, j, k: (i, j)) for _ in outs]
    out_shape = [jax.ShapeDtypeStruct((m, n_cols), dt) for dt in outs]
    blk += sum(_nbytes((tm, tn), dt) for dt in outs)
    operands = [x, *([w] * len(w_col_blocks)), *[e[0] for e in extras]]
    for src, layer in casts:
        _, n_rows, n_src_cols = src.shape
        rows = _cast_chunk_rows(n_rows, grid[0] * grid[1] * grid[2])
        chunk = lambda i, j, k, last=n_rows // rows - 1: jnp.minimum(
            (i * grid[1] + j) * grid[2] + k, last)
        in_specs.append(pl.BlockSpec((None, rows, n_src_cols),
                                     lambda i, j, k, layer=layer, chunk=chunk: (layer, chunk(i, j, k), 0)))
        out_specs.append(pl.BlockSpec((rows, n_src_cols), lambda i, j, k, chunk=chunk: (chunk(i, j, k), 0)))
        out_shape.append(jax.ShapeDtypeStruct((n_rows, n_src_cols), BF16))
        operands.append(src)
        blk += _nbytes((rows, n_src_cols), F32) + _nbytes((rows, n_src_cols), BF16)
    n_w = len(w_col_blocks)
    scratch = [pltpu.VMEM((tm, tn), F32) for _ in range(n_w)] if nk > 1 else []
    scratch_bytes = (len(scratch) + 3 * n_w) * _nbytes((tm, tn), F32)
    kern = functools.partial(_mm_kernel, n_w=n_w, n_e=len(extras), n_o=len(outs), nk=nk,
                             n_cast=len(casts), epilogue=epilogue)
    return pl.pallas_call(
        kern,
        grid=grid,
        in_specs=in_specs,
        out_specs=out_specs,
        out_shape=out_shape,
        scratch_shapes=scratch,
        compiler_params=pltpu.CompilerParams(
            dimension_semantics=("arbitrary", "arbitrary", "arbitrary"),
            vmem_limit_bytes=_vmem_limit(blk + scratch_bytes // 2)),
        name=name,
    )(*operands)


def _cast_kernel(src_ref, o_ref):
    o_ref[...] = src_ref[...].astype(o_ref.dtype)


def _cast_bf16(src, layer, *, rows=128):
    _, n_rows, n_cols = src.shape
    blk = _nbytes((rows, n_cols), F32) + _nbytes((rows, n_cols), BF16)
    return pl.pallas_call(
        _cast_kernel,
        grid=(n_rows // rows,),
        in_specs=[pl.BlockSpec((None, rows, n_cols), lambda r: (layer, r, 0))],
        out_specs=pl.BlockSpec((rows, n_cols), lambda r: (r, 0)),
        out_shape=jax.ShapeDtypeStruct((n_rows, n_cols), BF16),
        compiler_params=pltpu.CompilerParams(
            dimension_semantics=("arbitrary",),
            vmem_limit_bytes=_vmem_limit(blk)),
        name="cast_weights",
    )(src)


def _ep_swiglu(accs, j, e_refs, o_refs):
    a, b = accs
    o_refs[0][...] = (_silu(a) * b).astype(o_refs[0].dtype)


def _ep_residual(accs, j, e_refs, o_refs, *, coef):
    h_ref, gate_ref = e_refs
    o_refs[0][...] = h_ref[...] + (coef * gate_ref[0]) * accs[0]


def _ep_qkv(accs, j, e_refs, o_refs, *, n_norm_tiles):
    y = accs[0]
    gain_ref, = e_refs
    o_ref = o_refs[0]
    tn = y.shape[1]

    @pl.when(j < n_norm_tiles)
    def _():
        g = gain_ref[0]
        for hd in range(tn // HEAD_DIM):
            yh = y[:, hd * HEAD_DIM:(hd + 1) * HEAD_DIM]
            inv = lax.rsqrt(jnp.mean(yh * yh, axis=-1, keepdims=True) + EPS)
            o_ref[:, hd * HEAD_DIM:(hd + 1) * HEAD_DIM] = (yh * inv * g).astype(o_ref.dtype)

    @pl.when(j >= n_norm_tiles)
    def _():
        o_ref[...] = y.astype(o_ref.dtype)


def _ep_silu(accs, j, e_refs, o_refs):
    o_refs[0][...] = _silu(accs[0]).astype(o_refs[0].dtype)


def _ep_cast(accs, j, e_refs, o_refs):
    o_refs[0][...] = accs[0].astype(o_refs[0].dtype)


def _forget_constants(lbp, layer):
    depth = lbp.shape[0]
    mx = lbp[0:1]
    for l in range(1, depth):
        mx = jnp.maximum(mx, lbp[l:l + 1])
    es = [jnp.exp(lbp[l:l + 1] - mx) for l in range(depth)]
    tot = es[0]
    for l in range(1, depth):
        tot = tot + es[l]
    s = [e / tot for e in es]
    cum = s[0]
    for l in range(1, layer + 1):
        cum = cum + s[l]
    lb = cum - s[0]
    return jnp.log(lb) * LOG2_E, jnp.log1p(-lb) * LOG2_E, 1.0 - lb


def _forget_gate(f, a2, c2, one_m_lb):
    f2 = f * LOG2_E
    t = jnp.exp2(-jnp.abs(f2))
    one_t = 1.0 + t
    b2 = (c2 + jnp.minimum(f2, 0.0)) - jnp.log(one_t) * LOG2_E
    e = jnp.exp2(-jnp.abs(a2 - b2))
    log2_f = jnp.maximum(a2, b2) + jnp.log(1.0 + e) * LOG2_E
    return log2_f, one_m_lb * jnp.where(f >= 0.0, t, 1.0) / one_t


def _split3(x):
    hi = x.astype(BF16)
    r = x - hi.astype(F32)
    mid = r.astype(BF16)
    lo = (r - mid.astype(F32)).astype(BF16)
    return hi, mid, lo


def _attn_tables(sinks, *, n_kv, group, blk):
    n_heads = n_kv * group
    q_feat = jnp.zeros((n_heads, blk, LANES), BF16)
    for i, s_i in enumerate(_split3(sinks)):
        q_feat = q_feat.at[:, :, i].set(jnp.broadcast_to(s_i[:, None], (n_heads, blk)))
    q_feat = q_feat.reshape(n_kv, group * blk, LANES)
    k_feat = np.zeros((2 * blk, LANES), np.float32)
    k_feat[0, 0:3] = 1.0
    qi = np.arange(blk)[:, None]
    kj = np.arange(2 * blk)[None, :]
    dist = (qi + blk - kj).astype(np.float32)
    valid = (dist >= 0) & (dist < WINDOW)
    sink_col = kj == 0
    inf = np.float32(np.inf)
    dist_all = np.where(sink_col, np.float32(0), np.where(valid, dist, inf))
    dist_first = np.where(sink_col, np.float32(0), np.where(valid & (kj >= blk), dist, inf))
    return q_feat, jnp.asarray(k_feat, BF16), jnp.asarray(dist_all), jnp.asarray(dist_first)


def _attn_kernel(slope_ref, qf_ref, kf_ref, dist_ref, dist_first_ref, q_ref, kc_ref, kp_ref, vc_ref,
                 vp_ref, o_ref, d_scr, *, n_kv, group):
    blk = q_ref.shape[0]
    d_scr[...] = jnp.where(pl.program_id(1) > 0, dist_ref[...], dist_first_ref[...])
    keep = (lax.broadcasted_iota(jnp.int32, (16, HEAD_DIM), 0) != 0).astype(BF16)
    ones = jnp.ones((2 * blk, HEAD_DIM), BF16)
    k_feat = kf_ref[...]

    def kv_head(hk, carry):
        ko = pl.ds(pl.multiple_of(hk * HEAD_DIM, HEAD_DIM), HEAD_DIM)
        kp = kp_ref[:, ko]
        vp = vp_ref[:, ko]
        vp = jnp.concatenate([vp[:16] * keep, vp[16:]], axis=0)
        kp = jnp.concatenate([kp[:16] * keep, kp[16:]], axis=0)
        k2 = jnp.concatenate([jnp.concatenate([kp, kc_ref[:, ko]], axis=0), k_feat], axis=1)
        v2 = jnp.concatenate([jnp.concatenate([vp, vc_ref[:, ko]], axis=0), ones], axis=1)
        cols = [pl.ds(pl.multiple_of((hk * group + g) * HEAD_DIM, HEAD_DIM), HEAD_DIM)
                for g in range(group)]
        q4 = jnp.concatenate([q_ref[:, cg] for cg in cols], axis=0)
        q4 = jnp.concatenate([q4, qf_ref[hk]], axis=1)
        s = _dot_nt(q4, k2)
        dist = d_scr[...]
        logits = jnp.concatenate(
            [s[g * blk:(g + 1) * blk] - slope_ref[hk * group + g] * dist for g in range(group)],
            axis=0)
        m = jnp.max(logits, axis=-1, keepdims=True)
        p = jnp.exp(logits - m).astype(BF16)
        o = _dot(p, v2)
        out = o[:, :HEAD_DIM] / o[:, HEAD_DIM:]
        for g, cg in enumerate(cols):
            o_ref[:, cg] = out[g * blk:(g + 1) * blk].astype(o_ref.dtype)
        return carry

    lax.fori_loop(0, n_kv, kv_head, 0, unroll=True)


def _attention(qkv, slopes, sinks, *, batch, seq, n_heads, n_kv):
    m = qkv.shape[0]
    blk = WINDOW
    group = n_heads // n_kv
    nb = seq // blk
    dq = n_heads * HEAD_DIM
    dkv = n_kv * HEAD_DIM
    k_col = dq // dkv
    v_col = k_col + 1
    q_feat, k_feat, dist, dist_first = _attn_tables(sinks, n_kv=n_kv, group=group, blk=blk)
    cur = lambda b, n: b * nb + n
    prev = lambda b, n: b * nb + jnp.maximum(n - 1, 0)
    const2 = lambda b, n: (0, 0)
    blk_bytes = (2 * _nbytes((blk, dq), BF16) + 4 * _nbytes((blk, dkv), BF16)
                 + _nbytes(q_feat.shape, BF16) + 3 * _nbytes(dist.shape, F32))
    return pl.pallas_call(
        functools.partial(_attn_kernel, n_kv=n_kv, group=group),
        grid=(batch, nb),
        in_specs=[
            pl.BlockSpec(memory_space=pltpu.SMEM),
            pl.BlockSpec(q_feat.shape, lambda b, n: (0, 0, 0)),
            pl.BlockSpec(k_feat.shape, const2),
            pl.BlockSpec(dist.shape, const2),
            pl.BlockSpec(dist.shape, const2),
            pl.BlockSpec((blk, dq), lambda b, n: (cur(b, n), 0)),
            pl.BlockSpec((blk, dkv), lambda b, n: (cur(b, n), k_col)),
            pl.BlockSpec((blk, dkv), lambda b, n: (prev(b, n), k_col)),
            pl.BlockSpec((blk, dkv), lambda b, n: (cur(b, n), v_col)),
            pl.BlockSpec((blk, dkv), lambda b, n: (prev(b, n), v_col)),
        ],
        out_specs=pl.BlockSpec((blk, dq), lambda b, n: (cur(b, n), 0)),
        out_shape=jax.ShapeDtypeStruct((m, dq), BF16),
        scratch_shapes=[pltpu.VMEM(dist.shape, F32)],
        compiler_params=pltpu.CompilerParams(
            dimension_semantics=("arbitrary", "arbitrary"),
            vmem_limit_bytes=_vmem_limit(blk_bytes)),
        name="swa_attention",
    )(slopes, q_feat, k_feat, dist, dist_first, qkv, qkv, qkv, qkv, qkv)


def _hgrn_tables(c):
    halves = [c >> (lv + 1) for lv in range(int(np.log2(c)))]
    r = np.arange(c)[:, None]
    j = np.arange(c)[None, :]

    def partial_sums(half):
        ref = (r // (2 * half)) * (2 * half) + half - 1
        return np.where(r > ref, (j > ref) & (j <= r), (j > r) & (j <= ref))

    masks = [((r // (2 * h)) == (j // (2 * h))) & ((r % (2 * h)) >= h) & ((j % (2 * h)) < h)
             for h in halves]
    masks.append(r == j)
    w = np.concatenate([j <= r, partial_sums(4), partial_sums(2)], axis=0).astype(np.float32)
    w3 = np.concatenate([w, w, w], axis=1)
    return jnp.asarray(w3, BF16), jnp.asarray(np.stack(masks).astype(np.float32), BF16), halves


def _hgrn_kernel(w3_ref, mask_ref, gain_ref, lbp_ref, q_ref, f_ref, v_ref, gate_ref, o_ref,
                 s_ref, d_scr, lf_scr, k_scr, *, c, halves, layer):
    @pl.when(pl.program_id(2) == 0)
    def _():
        s_ref[...] = jnp.zeros_like(s_ref)

    n_chunks = q_ref.shape[0] // c
    zero_h = jnp.zeros((c, LANES), BF16)
    n_lv = len(halves)
    gate_consts = _forget_constants(lbp_ref[...], layer)

    def block_diag(x0, x1):
        return jnp.concatenate([jnp.concatenate([x0, zero_h], axis=1),
                                jnp.concatenate([zero_h, x1], axis=1)], axis=0)

    def both(x):
        return x[:, :LANES], x[:, LANES:]

    def pair_mask(i):
        mk = mask_ref[i]
        return jnp.concatenate([mk, mk], axis=1)

    def chunk(ci, carry):
        rows = pl.ds(pl.multiple_of(ci * c, c), c)
        q = lambda: q_ref[rows, :]
        log2_f, key = _forget_gate(f_ref[rows, :], *gate_consts)
        lf_scr[rows, :] = log2_f
        k_scr[rows, :] = key.astype(k_scr.dtype)
        k = lambda: k_scr[rows, :]
        lf3 = jnp.concatenate(_split3(lf_scr[rows, :]), axis=0)
        d_scr[...] = _dot(w3_ref[...], lf3)
        b = lambda: d_scr[0:c, :]

        def level_exponent(half):
            pieces = []
            for g0 in range(0, c, 2 * half):
                ref = jnp.broadcast_to(d_scr[g0 + half - 1:g0 + half, :], (half, 2 * LANES))
                pieces += [ref - d_scr[g0:g0 + half, :], d_scr[g0 + half:g0 + 2 * half, :] - ref]
            return jnp.concatenate(pieces, axis=0)

        e1 = jnp.exp2(lf_scr[rows, :]).astype(BF16)
        p = _dot_nt(jnp.concatenate([q(), q() * e1], axis=0), block_diag(*both(k())))
        a = p[0:c].astype(BF16) * pair_mask(n_lv) + p[c:2 * c].astype(BF16) * pair_mask(n_lv - 1)
        for li, half in enumerate(halves[:-1]):
            if half >= 8:
                dn = level_exponent(half)
            else:
                dn = d_scr[c:2 * c, :] if half == 4 else d_scr[2 * c:3 * c, :]
            e = jnp.exp2(dn).astype(BF16)
            a = a + _dot_nt(q() * e, block_diag(*both(k() * e))).astype(BF16) * pair_mask(li)
        intra = _dot(a, block_diag(*both(v_ref[rows, :])))

        s0 = s_ref[0]
        s1 = s_ref[1]
        inter = _dot_nt(q() * jnp.exp2(b()).astype(BF16),
                        block_diag(s0.astype(BF16), s1.astype(BF16)))
        o = inter + intra

        b_last = d_scr[c - 1:c, :]
        u = _dot_tn(v_ref[rows, :], k() * jnp.exp2(b_last - b()).astype(BF16))
        decay = jnp.exp2(b_last)
        s_ref[0] = s0 * decay[:, :LANES] + u[:LANES, :LANES]
        s_ref[1] = s1 * decay[:, LANES:] + u[LANES:, LANES:]

        gate = _silu(gate_ref[rows, :].astype(F32))
        gain = gain_ref[...]
        for hh in range(2):
            lanes = slice(hh * LANES, (hh + 1) * LANES)
            oh = o[:, lanes]
            y = oh * lax.rsqrt(jnp.mean(oh * oh, axis=-1, keepdims=True) + EPS) * gain
            o_ref[rows, lanes] = (y * gate[:, lanes]).astype(o_ref.dtype)
        return carry

    lax.fori_loop(0, n_chunks, chunk, 0, unroll=min(n_chunks, 16))


def _hgrn_recurrence(q, f, v, gate, gain, lower_bounds, *, layer, batch, seq, tb=4096):
    m, d = q.shape
    c = HGRN_CHUNK
    w3, masks, halves = _hgrn_tables(c)
    tb = min(tb, seq)
    nt = seq // tb
    width = 2 * HEAD_DIM
    tile = pl.BlockSpec((tb, width), lambda b, h, t: (b * nt + t, h))
    const2 = lambda b, h, t: (0, 0)
    depth = lower_bounds.shape[0]
    blk = (_nbytes((tb, width), F32) + 4 * _nbytes((tb, width), BF16)
           + _nbytes(w3.shape, BF16) + _nbytes(masks.shape, BF16)
           + (_nbytes((tb, width), F32) + _nbytes((tb, width), BF16)) // 2)
    return pl.pallas_call(
        functools.partial(_hgrn_kernel, c=c, halves=halves, layer=layer),
        grid=(batch, d // width, nt),
        in_specs=[
            pl.BlockSpec(w3.shape, const2),
            pl.BlockSpec(masks.shape, lambda b, h, t: (0, 0, 0)),
            pl.BlockSpec((1, HEAD_DIM), const2),
            pl.BlockSpec((depth, width), lambda b, h, t: (0, h)),
            tile, tile, tile, tile,
        ],
        out_specs=tile,
        out_shape=jax.ShapeDtypeStruct((m, d), BF16),
        scratch_shapes=[pltpu.VMEM((2, HEAD_DIM, HEAD_DIM), F32), pltpu.VMEM((3 * c, width), F32),
                        pltpu.VMEM((tb, width), F32), pltpu.VMEM((tb, width), BF16)],
        compiler_params=pltpu.CompilerParams(
            dimension_semantics=("arbitrary", "arbitrary", "arbitrary"),
            vmem_limit_bytes=_vmem_limit(blk)),
        name="hgrn2_recurrence",
    )(w3, masks, gain.reshape(1, HEAD_DIM), lower_bounds, q, f, v, gate)


def _bcast_mod(mod_l, idx, batch):
    d = mod_l.shape[-1] // N_MOD
    return mod_l[:batch, idx * d:(idx + 1) * d].reshape(batch, 1, d)


def _gate_up(u, w_gu, casts):
    f = w_gu.shape[1] // 2
    tn = SWIGLU_TILE_N
    return _mm(u, w_gu, [0, f // tn], f, [], [BF16], _ep_swiglu, name="ffn_gate_up",
               casts=casts, tn=tn)


def _residual_mm(x, w, h, gate, coef, *, rows_per_batch, name):
    tm, tn = TILE_M, TILE_N
    tpb = rows_per_batch // tm
    tk = min(FULL_K, x.shape[1])
    for k0 in range(x.shape[1] // tk):
        extras = [
            (h, (tm, tn), lambda i, j, k: (i, j)),
            (gate, (1, 1, tn), lambda i, j, k: (i // tpb, 0, j)),
        ]
        h, = _mm(x, w, [0], w.shape[1], extras, [F32], functools.partial(_ep_residual, coef=coef),
                 name=name, k_slice=(k0, 1), tm=tm, tn=tn, tk=tk)
    return h


def kernel(x, c, ada_w, ada_b, norm_gains, ffn1_w_gu, ffn1_w_down, ffn2_w_gu, ffn2_w_down,
           attn_w_qkv, attn_w_o, attn_q_gain, attn_k_gain, attn_sinks,
           hgrn_w_qfig, hgrn_w_o, hgrn_g_gain, hgrn_lower_bounds):
    batch, seq, d = x.shape
    depth = ada_w.shape[0]
    m = batch * seq
    n_mixers = 2

    c_pad = jnp.zeros((8, d), F32).at[:batch].set(c)
    mod = _ada(c_pad, ada_w, ada_b)

    w_gu = _cast_bf16(ffn1_w_gu, 0)
    h = x.reshape(m, d)
    for i in range(depth):
        md = lambda idx: _bcast_mod(mod[i], idx, batch)
        jm = i // n_mixers
        is_attn = i % n_mixers == 0
        u = _norm_mod(h, norm_gains[i, 0], md(0), md(1), rows_per_batch=seq)
        mixer_w = (attn_w_qkv, attn_w_o) if is_attn else (hgrn_w_qfig, hgrn_w_o)
        act, w_down, w_mix, w_o, w_gu = _gate_up(
            u, w_gu, [(ffn1_w_down, i), (mixer_w[0], jm), (mixer_w[1], jm), (ffn2_w_gu, i)])
        h = _residual_mm(act, w_down, h, md(2), 0.5, rows_per_batch=seq, name="ffn_down")
        u = _norm_mod(h, norm_gains[i, 1], md(3), md(4), rows_per_batch=seq)
        if is_attn:
            n_heads = attn_sinks.shape[1]
            n_qkv = w_mix.shape[1]
            n_kv = (n_qkv // HEAD_DIM - n_heads) // 2
            tn = n_kv * HEAD_DIM
            q_tiles = n_heads // n_kv
            ones = jnp.ones((1, HEAD_DIM), F32)
            gains = jnp.stack([attn_q_gain[jm].reshape(1, HEAD_DIM) * HEAD_DIM ** -0.5] * q_tiles
                              + [attn_k_gain[jm].reshape(1, HEAD_DIM), ones])
            qkv, = _mm(u, w_mix, [0], n_qkv,
                       [(gains, (1, 1, HEAD_DIM), lambda i_, j_, k_: (j_, 0, 0))], [BF16],
                       functools.partial(_ep_qkv, n_norm_tiles=q_tiles + 1),
                       name="attn_qkv", tn=tn)
            heads = jnp.arange(1, n_heads + 1, dtype=F32)
            slopes = jnp.exp2(-8.0 * heads / n_heads)
            y = _attention(qkv, slopes, attn_sinks[jm].astype(F32), batch=batch, seq=seq,
                           n_heads=n_heads, n_kv=n_kv)
            mixer_name = "attn_out"
        else:
            fd = (w_mix.shape[1] - 2 * d) // 2
            tn = TILE_N
            qh, = _mm(u, w_mix, [0], fd, [], [BF16], _ep_silu, name="hgrn_q", tn=tn)
            fh, = _mm(u, w_mix, [fd // tn], fd, [], [F32], _ep_cast, name="hgrn_f", tn=tn)
            vh, = _mm(u, w_mix, [2 * fd // tn], d, [], [BF16], _ep_cast, name="hgrn_i", tn=tn)
            gh, = _mm(u, w_mix, [(2 * fd + d) // tn], d, [], [BF16], _ep_cast, name="hgrn_g", tn=tn)
            y = _hgrn_recurrence(qh, fh, vh, gh, hgrn_g_gain[jm], hgrn_lower_bounds.astype(F32),
                                 layer=i, batch=batch, seq=seq)
            mixer_name = "hgrn_out"
        h = _residual_mm(y, w_o, h, md(5), 1.0, rows_per_batch=seq, name=mixer_name)
        u = _norm_mod(h, norm_gains[i, 2], md(6), md(7), rows_per_batch=seq)
        later = [(ffn2_w_down, i)] + ([(ffn1_w_gu, i + 1)] if i + 1 < depth else [])
        act, w_down, *next_gu = _gate_up(u, w_gu, later)
        h = _residual_mm(act, w_down, h, md(8), 0.5, rows_per_batch=seq, name="ffn_down")
        if next_gu:
            w_gu, = next_gu
    return h.reshape(batch, seq, d)
```
